```python
import math
import jax, jax.numpy as jnp
from jax import lax
import numpy as np

D_MODEL = 1024
BATCH = 4
SEQ = 8192
DEPTH = 1

ATTN_HEADS = 8
ATTN_HEAD_DIM = 64
ATTN_WIDTH = ATTN_HEADS * ATTN_HEAD_DIM
GMLP_GROUPS = 8
GMLP_GROUP_DIM = 64
GMLP_WIDTH = GMLP_GROUPS * GMLP_GROUP_DIM
MIX_WIDTH = ATTN_WIDTH + GMLP_WIDTH
IN_COLS = 3 * ATTN_WIDTH + 2 * GMLP_WIDTH
Q_BLOCK = 128
GMLP_CHUNK = 128
PEER_HEADS = 8
PEER_N_KEYS = 128
PEER_N_EXPERTS = PEER_N_KEYS * PEER_N_KEYS
PEER_QUERY_DIM = 256
PEER_HALF = PEER_QUERY_DIM // 2
PEER_TOPK = 16
PEER_TOKEN_BLOCK = 128
PLE_DIM = 256
LN_EPS = 1e-5
DEEPNORM_ALPHA = (2.0 * DEPTH) ** 0.25
DEEPNORM_BETA = (8.0 * DEPTH) ** -0.25

kernel_name = "hybrid_sb_attn_gmlp_peer_deepnorm"


def layer_norm(x, g, b):
    xf = x.astype(jnp.float32)
    mu = jnp.mean(xf, axis=-1, keepdims=True)
    var = jnp.mean(jnp.square(xf - mu), axis=-1, keepdims=True)
    y = (xf - mu) * lax.rsqrt(var + LN_EPS)
    return (y * g.astype(jnp.float32) + b.astype(jnp.float32)).astype(x.dtype)


def rms_norm(x, g):
    xf = x.astype(jnp.float32)
    y = xf * lax.rsqrt(jnp.mean(jnp.square(xf), axis=-1, keepdims=True) + LN_EPS)
    return (y * g.astype(jnp.float32)).astype(x.dtype)


def stick_breaking_attention(q, k, v):
    B, S, H, dh = q.shape
    nb = S // Q_BLOCK
    qh = q.transpose(0, 2, 1, 3)
    kh = k.transpose(0, 2, 1, 3)
    vh = v.transpose(0, 2, 1, 3)
    q_blocks = qh.reshape(B, H, nb, Q_BLOCK, dh).transpose(2, 0, 1, 3, 4)
    starts = jnp.arange(nb, dtype=jnp.int32) * Q_BLOCK
    scale = 1.0 / math.sqrt(dh)
    s_idx = jnp.arange(S, dtype=jnp.int32)

    def block(args):
        qb, start = args
        z = jnp.einsum('bhtd,bhsd->bhts', qb, kh).astype(jnp.float32) * scale
        t_idx = start + jnp.arange(Q_BLOCK, dtype=jnp.int32)
        mask = s_idx[None, :] < t_idx[:, None]
        log_beta = jax.nn.log_sigmoid(z)
        log_1m_beta = jnp.where(mask, jax.nn.log_sigmoid(-z), 0.0)
        suffix = lax.cumsum(log_1m_beta, axis=3, reverse=True) - log_1m_beta
        a = jnp.where(mask, jnp.exp(log_beta + suffix), 0.0)
        return jnp.einsum('bhts,bhsd->bhtd', a.astype(vh.dtype), vh)

    out = lax.map(block, (q_blocks, starts))
    return out.transpose(1, 0, 3, 2, 4).reshape(B, S, H * dh)


def gmlp_spatial_gating(u, v, vn_g, vn_b, w_s, b_s):
    B, S, _ = v.shape
    nc = S // GMLP_CHUNK
    v = layer_norm(v, vn_g, vn_b)
    vc = v.reshape(B, nc, GMLP_CHUNK, GMLP_GROUPS, GMLP_GROUP_DIM)
    causal = jnp.tril(jnp.ones((GMLP_CHUNK, GMLP_CHUNK), dtype=w_s.dtype))
    mixed = jnp.einsum('gts,bcsgd->bctgd', w_s * causal[None], vc)
    mixed = mixed + b_s.T[None, None, :, :, None]
    return u * mixed.reshape(B, S, GMLP_WIDTH)


def peer(x, w_q, sub_keys, u_tab, v_tab):
    B, S, D = x.shape
    q = (x @ w_q).reshape(B, S, PEER_HEADS, 2, PEER_HALF)
    scores = jnp.einsum('bshpk,hpnk->bshpn', q, sub_keys).astype(jnp.float32)
    s1, i1 = lax.top_k(scores[:, :, :, 0, :], PEER_TOPK)
    s2, i2 = lax.top_k(scores[:, :, :, 1, :], PEER_TOPK)
    cand = (s1[..., :, None] + s2[..., None, :]).reshape(B, S, PEER_HEADS, PEER_TOPK * PEER_TOPK)
    top_s, top_c = lax.top_k(cand, PEER_TOPK)
    ia = top_c // PEER_TOPK
    ib = top_c % PEER_TOPK
    ids = (jnp.take_along_axis(i1, ia, axis=-1) * PEER_N_KEYS
           + jnp.take_along_axis(i2, ib, axis=-1))
    gates = jax.nn.softmax(top_s, axis=-1).astype(x.dtype)

    T = B * S
    nblk = T // PEER_TOKEN_BLOCK
    xt = x.reshape(nblk, PEER_TOKEN_BLOCK, D)
    idt = ids.reshape(nblk, PEER_TOKEN_BLOCK, PEER_HEADS, PEER_TOPK)
    gt = gates.reshape(nblk, PEER_TOKEN_BLOCK, PEER_HEADS, PEER_TOPK)

    def block(args):
        xb, idb, gb = args
        ue = jnp.take(u_tab, idb, axis=0)
        h = jnp.einsum('thkd,td->thk', ue, xb)
        act = gb * jax.nn.gelu(h, approximate=False)
        ve = jnp.take(v_tab, idb, axis=0)
        return jnp.einsum('thk,thkd->td', act, ve)

    out = lax.map(block, (xt, idt, gt))
    return out.reshape(B, S, D)


def setup_inputs(seed: int = 0) -> dict:
    key = jax.random.key(seed)
    ks = jax.random.split(key, 24)
    f32 = jnp.float32
    n = lambda k, shape, s: jax.random.normal(k, shape, f32) * s
    L = DEPTH
    return {
        "x": n(ks[0], (BATCH, SEQ, D_MODEL), 1.0),
        "p": n(ks[1], (DEPTH, BATCH, SEQ, PLE_DIM), 1.0),
        "w_in": n(ks[2], (L, D_MODEL, IN_COLS), D_MODEL ** -0.5),
        "attn_out_g": 1.0 + n(ks[3], (L, ATTN_WIDTH), 0.05),
        "gmlp_vn_g": 1.0 + n(ks[4], (L, GMLP_WIDTH), 0.05),
        "gmlp_vn_b": n(ks[5], (L, GMLP_WIDTH), 0.02),
        "gmlp_ws": n(ks[6], (L, GMLP_GROUPS, GMLP_CHUNK, GMLP_CHUNK), GMLP_CHUNK ** -0.5),
        "gmlp_bs": 1.0 + n(ks[7], (L, GMLP_GROUPS, GMLP_CHUNK), 0.1),
        "gmlp_out_g": 1.0 + n(ks[8], (L, GMLP_WIDTH), 0.05),
        "w_out": n(ks[9], (L, MIX_WIDTH, D_MODEL), MIX_WIDTH ** -0.5 * DEEPNORM_BETA),
        "ln1_g": 1.0 + n(ks[10], (L, D_MODEL), 0.05),
        "ln1_b": n(ks[11], (L, D_MODEL), 0.02),
        "peer_wq": n(ks[12], (L, D_MODEL, PEER_HEADS * PEER_QUERY_DIM), D_MODEL ** -0.5),
        "peer_subkeys": n(ks[13], (L, PEER_HEADS, 2, PEER_N_KEYS, PEER_HALF), PEER_HALF ** -0.5),
        "peer_u": n(ks[14], (L, PEER_N_EXPERTS, D_MODEL), D_MODEL ** -0.5),
        "peer_v": n(ks[15], (L, PEER_N_EXPERTS, D_MODEL), DEEPNORM_BETA),
        "ln2_g": 1.0 + n(ks[16], (L, D_MODEL), 0.05),
        "ln2_b": n(ks[17], (L, D_MODEL), 0.02),
        "ple_wproj": n(ks[18], (L, PLE_DIM, D_MODEL), PLE_DIM ** -0.5 * DEEPNORM_BETA),
        "ple_wgate": n(ks[19], (L, D_MODEL, D_MODEL), D_MODEL ** -0.5),
        "ple_bgate": n(ks[20], (L, D_MODEL), 0.02),
        "ln3_g": 1.0 + n(ks[21], (L, D_MODEL), 0.05),
        "ln3_b": n(ks[22], (L, D_MODEL), 0.02),
    }


def reference(x, p, w_in, attn_out_g, gmlp_vn_g, gmlp_vn_b, gmlp_ws, gmlp_bs, gmlp_out_g, w_out,
              ln1_g, ln1_b, peer_wq, peer_subkeys, peer_u, peer_v, ln2_g, ln2_b,
              ple_wproj, ple_wgate, ple_bgate, ln3_g, ln3_b):
    B, S, D = x.shape
    for i in range(DEPTH):
        h = x @ w_in[i]
        q, k, v, gu, gv = jnp.split(
            h, [ATTN_WIDTH, 2 * ATTN_WIDTH, 3 * ATTN_WIDTH, 3 * ATTN_WIDTH + GMLP_WIDTH], axis=-1)
        hs = (B, S, ATTN_HEADS, ATTN_HEAD_DIM)
        attn = stick_breaking_attention(q.reshape(hs), k.reshape(hs), v.reshape(hs))
        attn = rms_norm(attn, attn_out_g[i])
        gm = gmlp_spatial_gating(jax.nn.gelu(gu, approximate=False), jax.nn.gelu(gv, approximate=False),
                                 gmlp_vn_g[i], gmlp_vn_b[i], gmlp_ws[i], gmlp_bs[i])
        gm = rms_norm(gm, gmlp_out_g[i])
        mix = jnp.concatenate([attn, gm], axis=-1) @ w_out[i]
        x = layer_norm(DEEPNORM_ALPHA * x + mix, ln1_g[i], ln1_b[i])
        ffn = peer(x, peer_wq[i], peer_subkeys[i], peer_u[i], peer_v[i])
        x = layer_norm(DEEPNORM_ALPHA * x + ffn, ln2_g[i], ln2_b[i])
        gate = jax.nn.sigmoid(x @ ple_wgate[i] + ple_bgate[i])
        ple = gate * (p[i] @ ple_wproj[i])
        x = layer_norm(DEEPNORM_ALPHA * x + ple, ln3_g[i], ln3_b[i])
    return x
```

```python
import functools
import math

import jax
import jax.numpy as jnp
from jax import lax
from jax.experimental import pallas as pl
from jax.experimental.pallas import tpu as pltpu

F32 = jnp.float32
BF16 = jnp.bfloat16

ATTN_HEADS = 8
ATTN_HEAD_DIM = 64
ATTN_WIDTH = ATTN_HEADS * ATTN_HEAD_DIM
GMLP_GROUPS = 8
GMLP_GROUP_DIM = 64
GMLP_WIDTH = GMLP_GROUPS * GMLP_GROUP_DIM
GMLP_CHUNK = 128
PEER_HEADS = 8
PEER_N_KEYS = 128
PEER_TOPK = 16
LN_EPS = 1e-5

LANES = 128
SUBLANES = 8
VMEM_LIMIT = 56 * 1024 * 1024

EXP_ZERO_BELOW = -110.0


def _gelu(x):
    return 0.5 * x * (1.0 + lax.erf(x * (1.0 / math.sqrt(2.0))))


def _layer_norm(x, g, b):
    mu = jnp.mean(x, axis=-1, keepdims=True)
    xc = x - mu
    var = jnp.mean(xc * xc, axis=-1, keepdims=True)
    return xc * lax.rsqrt(var + LN_EPS) * g + b


def _rms_norm(x, g):
    return x * lax.rsqrt(jnp.mean(x * x, axis=-1, keepdims=True) + LN_EPS) * g


def _params(*sem):
    return pltpu.CompilerParams(dimension_semantics=sem, vmem_limit_bytes=VMEM_LIMIT)


def _inproj_kernel(x_ref, w_ref, o_ref):
    o_ref[...] = jnp.dot(x_ref[...].astype(BF16), w_ref[...],
                         preferred_element_type=F32).astype(o_ref.dtype)


def _inproj(x, w):
    t, d = x.shape
    n = w.shape[1]
    tm = min(512, t)
    return pl.pallas_call(
        _inproj_kernel,
        grid=(t // tm,),
        in_specs=[pl.BlockSpec((tm, d), lambda i: (i, 0)),
                  pl.BlockSpec((d, n), lambda i: (0, 0))],
        out_specs=pl.BlockSpec((tm, n), lambda i: (i, 0)),
        out_shape=jax.ShapeDtypeStruct((t, n), BF16),
        compiler_params=_params("parallel"),
        name="inproj",
    )(x, w)


def _attn_kernel(q_ref, k_ref, v_ref, o_ref, c_ref, acc_ref, *, tq, tk):
    qi = pl.program_id(2)
    q = q_ref[0] * jnp.asarray(1.0 / math.sqrt(ATTN_HEAD_DIM), BF16)
    lane = lax.broadcasted_iota(jnp.int32, (tq, LANES), 1)
    is_h0 = lane < ATTN_HEAD_DIM
    zero = jnp.zeros_like(q)
    q_heads = (jnp.where(is_h0, q, zero), jnp.where(is_h0, zero, q))
    row = lax.broadcasted_iota(jnp.int32, (tq, tk), 0) + qi * tq
    col = lax.broadcasted_iota(jnp.int32, (tq, tk), 1)
    rj = lax.broadcasted_iota(jnp.int32, (tk, tk + LANES), 0)
    cs = lax.broadcasted_iota(jnp.int32, (tk, tk + LANES), 1)
    cum = jnp.where(cs >= tk, 1.0, jnp.where(rj > cs, 1.0, 0.0)).astype(BF16)

    c_ref[...] = jnp.zeros_like(c_ref)
    acc_ref[...] = jnp.zeros_like(acc_ref)

    def body(carry):
        kb, _ = carry
        koff = pl.multiple_of(kb * tk, tk)
        k = k_ref[0, pl.ds(koff, tk), :]
        v = v_ref[0, pl.ds(koff, tk), :]
        mask = (col + kb * tk) < row
        for h in range(2):
            z = lax.dot_general(q_heads[h], k, (((1,), (1,)), ((), ())),
                                preferred_element_type=F32)
            sp = jnp.log1p(jnp.exp(-jnp.abs(z)))
            log_beta = jnp.minimum(z, 0.0) - sp
            l1m = jnp.where(mask, jnp.minimum(-z, 0.0) - sp, 0.0)
            l_hi = l1m.astype(BF16)
            l_lo = (l1m - l_hi.astype(F32)).astype(BF16)
            s = (jnp.dot(l_hi, cum, preferred_element_type=F32)
                 + jnp.dot(l_lo, cum, preferred_element_type=F32))
            c = c_ref[h]
            a = jnp.where(mask, jnp.exp(log_beta + s[:, :tk] + c), 0.0)
            acc_ref[h] += jnp.dot(a.astype(BF16), v, preferred_element_type=F32)
            c_ref[h] = c + s[:, tk:]
        return kb - 1, jnp.max(jnp.maximum(c_ref[0], c_ref[1]))

    def cond(carry):
        kb, cmax = carry
        return jnp.logical_and(kb >= 0, cmax > EXP_ZERO_BELOW)

    kb0 = ((qi + 1) * tq) // tk - 1
    lax.while_loop(cond, body, (kb0, jnp.zeros((), F32)))
    o_ref[0] = jnp.where(is_h0, acc_ref[0], acc_ref[1]).astype(o_ref.dtype)


def _attention(h3):
    b, s, _ = h3.shape
    tq = tk = LANES
    npairs = ATTN_WIDTH // LANES
    return pl.pallas_call(
        functools.partial(_attn_kernel, tq=tq, tk=tk),
        grid=(b, npairs, s // tq),
        in_specs=[pl.BlockSpec((1, tq, LANES), lambda bi, hp, qi: (bi, qi, hp)),
                  pl.BlockSpec((1, s, LANES), lambda bi, hp, qi: (bi, 0, npairs + hp)),
                  pl.BlockSpec((1, s, LANES), lambda bi, hp, qi: (bi, 0, 2 * npairs + hp))],
        out_specs=pl.BlockSpec((1, tq, LANES), lambda bi, hp, qi: (bi, qi, hp)),
        out_shape=jax.ShapeDtypeStruct((b, s, ATTN_WIDTH), BF16),
        scratch_shapes=[pltpu.VMEM((2, tq, LANES), F32), pltpu.VMEM((2, tq, LANES), F32)],
        compiler_params=_params("parallel", "parallel", "arbitrary"),
        name="sb_attention",
    )(h3, h3, h3)


def _gmlp_kernel(gu_ref, gv_ref, vng_ref, vnb_ref, ws_ref, bias_ref, og_ref, o_ref):
    u = _gelu(gu_ref[...].astype(F32))
    v = _gelu(gv_ref[...].astype(F32))
    v = _layer_norm(v, vng_ref[...], vnb_ref[...])
    r = lax.broadcasted_iota(jnp.int32, (GMLP_CHUNK, GMLP_CHUNK), 0)
    c = lax.broadcasted_iota(jnp.int32, (GMLP_CHUNK, GMLP_CHUNK), 1)
    causal = c <= r
    first_group = c < GMLP_GROUP_DIM
    cols = []
    for p in range(GMLP_WIDTH // LANES):
        vp = v[:, p * LANES:(p + 1) * LANES].astype(BF16)
        w0 = jnp.where(causal, ws_ref[2 * p], 0.0).astype(BF16)
        w1 = jnp.where(causal, ws_ref[2 * p + 1], 0.0).astype(BF16)
        m0 = jnp.dot(w0, vp, preferred_element_type=F32)
        m1 = jnp.dot(w1, vp, preferred_element_type=F32)
        cols.append(jnp.where(first_group, m0, m1))
    mixed = jnp.concatenate(cols, axis=1) + bias_ref[...]
    o_ref[...] = _rms_norm(u * mixed, og_ref[...]).astype(o_ref.dtype)


def _gmlp(h, vn_g, vn_b, w_s, b_s, out_g):
    t = h.shape[0]
    u_blk = (3 * ATTN_WIDTH) // GMLP_WIDTH
    bias = jnp.repeat(b_s.T, GMLP_GROUP_DIM, axis=1)
    row = lambda a: a.reshape(1, -1)
    full = lambda shape: pl.BlockSpec(shape, lambda i: (0,) * len(shape))
    return pl.pallas_call(
        _gmlp_kernel,
        grid=(t // GMLP_CHUNK,),
        in_specs=[pl.BlockSpec((GMLP_CHUNK, GMLP_WIDTH), lambda i: (i, u_blk)),
                  pl.BlockSpec((GMLP_CHUNK, GMLP_WIDTH), lambda i: (i, u_blk + 1)),
                  full((1, GMLP_WIDTH)), full((1, GMLP_WIDTH)),
                  full((GMLP_GROUPS, GMLP_CHUNK, GMLP_CHUNK)),
                  full((GMLP_CHUNK, GMLP_WIDTH)), full((1, GMLP_WIDTH))],
        out_specs=pl.BlockSpec((GMLP_CHUNK, GMLP_WIDTH), lambda i: (i, 0)),
        out_shape=jax.ShapeDtypeStruct((t, GMLP_WIDTH), BF16),
        compiler_params=_params("parallel"),
        name="gmlp_gating",
    )(h, h, row(vn_g), row(vn_b), w_s, bias, row(out_g))


def _outproj_kernel(attn_ref, gm_ref, x_ref, ag_ref, w_ref, g_ref, b_ref, x1_ref, x1t_ref, *, alpha):
    a = _rms_norm(attn_ref[...].astype(F32), ag_ref[...]).astype(BF16)
    mix = (jnp.dot(a, w_ref[:ATTN_WIDTH, :], preferred_element_type=F32)
           + jnp.dot(gm_ref[...], w_ref[ATTN_WIDTH:, :], preferred_element_type=F32))
    x1 = _layer_norm(alpha * x_ref[...] + mix, g_ref[...], b_ref[...])
    x1_ref[...] = x1
    x1t_ref[...] = x1.T.astype(BF16)


def _outproj(attn, gm, x, attn_g, w_out, ln_g, ln_b, alpha):
    t, d = x.shape
    tm = min(256, t)
    row = lambda a: a.reshape(1, -1)
    full = lambda shape: pl.BlockSpec(shape, lambda i: (0,) * len(shape))
    return pl.pallas_call(
        functools.partial(_outproj_kernel, alpha=alpha),
        grid=(t // tm,),
        in_specs=[pl.BlockSpec((tm, ATTN_WIDTH), lambda i: (i, 0)),
                  pl.BlockSpec((tm, GMLP_WIDTH), lambda i: (i, 0)),
                  pl.BlockSpec((tm, d), lambda i: (i, 0)),
                  full((1, ATTN_WIDTH)), full(w_out.shape), full((1, d)), full((1, d))],
        out_specs=[pl.BlockSpec((tm, d), lambda i: (i, 0)),
                   pl.BlockSpec((d, tm), lambda i: (0, i))],
        out_shape=[jax.ShapeDtypeStruct((t, d), F32), jax.ShapeDtypeStruct((d, t), BF16)],
        compiler_params=_params("parallel"),
        name="outproj_ln1",
    )(attn, gm, x, row(attn_g), w_out, row(ln_g), row(ln_b))


def _cmp_exchange(x, i, j):
    hi = jnp.maximum(x[i], x[j])
    lo = jnp.minimum(x[i], x[j])
    x[i], x[j] = hi, lo


def _bitonic_merge_desc(x):
    n = len(x)
    d = n // 2
    while d >= 1:
        for i in range(n):
            if i & d == 0:
                _cmp_exchange(x, i, i | d)
        d //= 2


def _bitonic_sort_desc(x):
    n = len(x)
    k = 2
    while k <= n:
        j = k // 2
        while j >= 1:
            for i in range(n):
                l = i ^ j
                if l > i:
                    if i & k == 0:
                        _cmp_exchange(x, i, l)
                    else:
                        _cmp_exchange(x, l, i)
            j //= 2
        k *= 2


def _merge_top(a, b):
    n = len(a)
    c = [jnp.maximum(a[k], b[n - 1 - k]) for k in range(n)]
    _bitonic_merge_desc(c)
    return c


def _peer_kernel(x1t_ref, x1_ref, wqt_ref, keys_ref, u_ref, vt_ref, g_ref, b_ref, o_ref,
                 s_ref, top_ref, tau_ref, m1_ref, m2_ref, rz_ref, e1_ref, e2_ref,
                 ht_ref, at_ref, acc_ref, *, tm, eb, alpha):
    j = pl.program_id(1)
    nlt = tm // LANES
    nhp = 2 * PEER_HEADS
    nv = PEER_N_KEYS // SUBLANES

    @pl.when(j == 0)
    def _route():
        qt = jnp.dot(wqt_ref[...], x1t_ref[...], preferred_element_type=F32).astype(BF16)
        for hp in range(nhp):
            s_ref[hp] = jnp.dot(keys_ref[hp], qt[hp * PEER_N_KEYS:(hp + 1) * PEER_N_KEYS, :],
                                preferred_element_type=F32)

        def sort_body(hp, carry):
            for lt in range(nlt):
                sl = slice(lt * LANES, (lt + 1) * LANES)
                x = [s_ref[hp, v * SUBLANES:(v + 1) * SUBLANES, sl] for v in range(nv)]
                _bitonic_sort_desc(x)
                for shift in (4, 2, 1):
                    x = _merge_top(x, [pltpu.roll(xi, shift, 0) for xi in x])
                for k in range(PEER_TOPK):
                    top_ref[hp, k:k + 1, sl] = x[k][0:1, :]
            return carry

        lax.fori_loop(0, nhp, sort_body, 0)

        sub = lax.broadcasted_iota(jnp.int32, (SUBLANES, LANES), 0)
        for lt in range(nlt):
            sl = slice(lt * LANES, (lt + 1) * LANES)

            def heads_on_sublanes(p, k):
                out = jnp.zeros((SUBLANES, LANES), F32)
                for h in range(PEER_HEADS):
                    out = jnp.where(sub == h, top_ref[2 * h + p, k:k + 1, sl], out)
                return out

            v1 = [heads_on_sublanes(0, k) for k in range(PEER_TOPK)]
            v2 = [heads_on_sublanes(1, k) for k in range(PEER_TOPK)]
            c = [v1[0] + v2[k] for k in range(PEER_TOPK)]
            for i in range(1, PEER_TOPK):
                c = _merge_top(c, [v1[i] + v2[k] for k in range(PEER_TOPK)])
            z = jnp.zeros((SUBLANES, LANES), F32)
            for k in range(PEER_TOPK):
                z = z + jnp.exp(c[k] - c[0])
            tau_ref[:, sl] = c[PEER_TOPK - 1]
            rz_ref[:, sl] = 1.0 / z
            m1_ref[:, sl] = v1[0]
            m2_ref[:, sl] = v2[0]

        for h in range(PEER_HEADS):
            e1_ref[h] = jnp.exp(s_ref[2 * h] - m1_ref[h:h + 1, :])
            e2_ref[h] = jnp.exp(s_ref[2 * h + 1] - m2_ref[h:h + 1, :]) * rz_ref[h:h + 1, :]
        acc_ref[...] = jnp.zeros_like(acc_ref)

    ht_ref[...] = jnp.dot(u_ref[...], x1t_ref[...], preferred_element_type=F32)
    na = eb // PEER_N_KEYS

    a0 = pl.multiple_of(j * na, SUBLANES)
    sub = lax.broadcasted_iota(jnp.int32, (SUBLANES, LANES), 0)

    def a_body(al, carry):
        r0 = pl.multiple_of(al * PEER_N_KEYS, PEER_N_KEYS)
        pick = sub == al

        def row(ref, idx, sl):
            return jnp.sum(jnp.where(pick, ref[idx, pl.ds(a0, SUBLANES), sl], 0.0),
                           axis=0, keepdims=True)

        for lt in range(nlt):
            sl = slice(lt * LANES, (lt + 1) * LANES)
            g = jnp.zeros((PEER_N_KEYS, LANES), F32)
            for h in range(PEER_HEADS):
                pair = row(s_ref, 2 * h, sl) + s_ref[2 * h + 1, :, sl]
                w = row(e1_ref, h, sl) * e2_ref[h, :, sl]
                g = g + jnp.where(pair >= tau_ref[h:h + 1, sl], w, 0.0)
            act = g * _gelu(ht_ref[pl.ds(r0, PEER_N_KEYS), sl])
            at_ref[pl.ds(r0, PEER_N_KEYS), sl] = act.astype(BF16)
        return carry

    lax.fori_loop(0, na, a_body, 0)
    acc_ref[...] += jnp.dot(vt_ref[...], at_ref[...], preferred_element_type=F32)

    @pl.when(j == pl.num_programs(1) - 1)
    def _finish():
        y = alpha * x1_ref[...] + acc_ref[...].T
        o_ref[...] = _layer_norm(y, g_ref[...], b_ref[...])


def _peer(x1t, x1, wqt, keys, u, vt, ln_g, ln_b, alpha):
    t, d = x1.shape
    e = u.shape[0]
    tm = min(512, t)
    eb = SUBLANES * PEER_N_KEYS
    nhp = 2 * PEER_HEADS
    row = lambda a: a.reshape(1, -1)
    full = lambda shape: pl.BlockSpec(shape, lambda i, j: (0,) * len(shape))
    return pl.pallas_call(
        functools.partial(_peer_kernel, tm=tm, eb=eb, alpha=alpha),
        grid=(t // tm, e // eb),
        in_specs=[pl.BlockSpec((d, tm), lambda i, j: (0, i)),
                  pl.BlockSpec((tm, d), lambda i, j: (i, 0)),
                  full(wqt.shape), full(keys.shape),
                  pl.BlockSpec((eb, d), lambda i, j: (j, 0)),
                  pl.BlockSpec((d, eb), lambda i, j: (0, j)),
                  full((1, d)), full((1, d))],
        out_specs=pl.BlockSpec((tm, d), lambda i, j: (i, 0)),
        out_shape=jax.ShapeDtypeStruct((t, d), F32),
        scratch_shapes=[pltpu.VMEM((nhp, PEER_N_KEYS, tm), F32),
                        pltpu.VMEM((nhp, PEER_TOPK, tm), F32),
                        pltpu.VMEM((PEER_HEADS, tm), F32),
                        pltpu.VMEM((PEER_HEADS, tm), F32),
                        pltpu.VMEM((PEER_HEADS, tm), F32),
                        pltpu.VMEM((PEER_HEADS, tm), F32),
                        pltpu.VMEM((PEER_HEADS, PEER_N_KEYS, tm), F32),
                        pltpu.VMEM((PEER_HEADS, PEER_N_KEYS, tm), F32),
                        pltpu.VMEM((eb, tm), F32),
                        pltpu.VMEM((eb, tm), BF16),
                        pltpu.VMEM((d, tm), F32)],
        compiler_params=_params("parallel", "arbitrary"),
        name="peer_ln2",
    )(x1t, x1, wqt, keys, u, vt, row(ln_g), row(ln_b))


def _ple_kernel(x_ref, p_ref, wg_ref, bg_ref, wp_ref, g_ref, b_ref, o_ref, *, alpha):
    x = x_ref[...]
    gate = jax.nn.sigmoid(jnp.dot(x.astype(BF16), wg_ref[...], preferred_element_type=F32)
                          + bg_ref[...])
    ple = gate * jnp.dot(p_ref[...].astype(BF16), wp_ref[...], preferred_element_type=F32)
    o_ref[...] = _layer_norm(alpha * x + ple, g_ref[...], b_ref[...])


def _ple(x, p, w_gate, b_gate, w_proj, ln_g, ln_b, alpha):
    t, d = x.shape
    tm = min(512, t)
    row = lambda a: a.reshape(1, -1)
    full = lambda shape: pl.BlockSpec(shape, lambda i: (0,) * len(shape))
    return pl.pallas_call(
        functools.partial(_ple_kernel, alpha=alpha),
        grid=(t // tm,),
        in_specs=[pl.BlockSpec((tm, d), lambda i: (i, 0)),
                  pl.BlockSpec((tm, p.shape[1]), lambda i: (i, 0)),
                  full(w_gate.shape), full((1, d)), full(w_proj.shape), full((1, d)), full((1, d))],
        out_specs=pl.BlockSpec((tm, d), lambda i: (i, 0)),
        out_shape=jax.ShapeDtypeStruct((t, d), F32),
        compiler_params=_params("parallel"),
        name="ple_ln3",
    )(x, p, w_gate, row(b_gate), w_proj, row(ln_g), row(ln_b))


def kernel(x, p, w_in, attn_out_g, gmlp_vn_g, gmlp_vn_b, gmlp_ws, gmlp_bs, gmlp_out_g, w_out, ln1_g, ln1_b, peer_wq, peer_subkeys, peer_u, peer_v, ln2_g, ln2_b, ple_wproj, ple_wgate, ple_bgate, ln3_g, ln3_b):
    b, s, d = x.shape
    depth = w_in.shape[0]
    t = b * s
    alpha = (2.0 * depth) ** 0.25
    xf = x.reshape(t, d)
    for i in range(depth):
        h = _inproj(xf, w_in[i].astype(BF16))
        attn = _attention(h.reshape(b, s, -1)).reshape(t, ATTN_WIDTH)
        gm = _gmlp(h, gmlp_vn_g[i], gmlp_vn_b[i], gmlp_ws[i], gmlp_bs[i], gmlp_out_g[i])
        x1, x1t = _outproj(attn, gm, xf, attn_out_g[i], w_out[i].astype(BF16), ln1_g[i], ln1_b[i], alpha)
        keys = peer_subkeys[i].reshape(2 * PEER_HEADS, PEER_N_KEYS, -1).astype(BF16)
        x2 = _peer(x1t, x1, peer_wq[i].T.astype(BF16), keys, peer_u[i].astype(BF16),
                   peer_v[i].T.astype(BF16), ln2_g[i], ln2_b[i], alpha)
        xf = _ple(x2, p[i].reshape(t, -1), ple_wgate[i].astype(BF16), ple_bgate[i],
                  ple_wproj[i].astype(BF16), ln3_g[i], ln3_b[i], alpha)
    return xf.reshape(b, s, d)
```

```python
import functools
import math

import jax
import jax.numpy as jnp
from jax import lax
from jax.experimental import pallas as pl
from jax.experimental.pallas import tpu as pltpu

F32 = jnp.float32
BF16 = jnp.bfloat16

ATTN_HEADS = 8
ATTN_HEAD_DIM = 64
ATTN_WIDTH = ATTN_HEADS * ATTN_HEAD_DIM
GMLP_GROUPS = 8
GMLP_GROUP_DIM = 64
GMLP_WIDTH = GMLP_GROUPS * GMLP_GROUP_DIM
GMLP_CHUNK = 128
PEER_HEADS = 8
PEER_N_KEYS = 128
PEER_TOPK = 16
LN_EPS = 1e-5

LANES = 128
SUBLANES = 8
VMEM_LIMIT = 56 * 1024 * 1024

EXP_ZERO_BELOW = -110.0


def _gelu(x):
    return 0.5 * x * (1.0 + lax.erf(x * (1.0 / math.sqrt(2.0))))


def _layer_norm(x, g, b):
    mu = jnp.mean(x, axis=-1, keepdims=True)
    xc = x - mu
    var = jnp.mean(xc * xc, axis=-1, keepdims=True)
    return xc * lax.rsqrt(var + LN_EPS) * g + b


def _rms_norm(x, g):
    return x * lax.rsqrt(jnp.mean(x * x, axis=-1, keepdims=True) + LN_EPS) * g


def _params(*sem):
    return pltpu.CompilerParams(dimension_semantics=sem, vmem_limit_bytes=VMEM_LIMIT)


def _inproj_kernel(x_ref, w_ref, o_ref):
    o_ref[...] = jnp.dot(x_ref[...].astype(BF16), w_ref[...],
                         preferred_element_type=F32).astype(o_ref.dtype)


def _inproj(x, w):
    t, d = x.shape
    n = w.shape[1]
    tm = min(512, t)
    return pl.pallas_call(
        _inproj_kernel,
        grid=(t // tm,),
        in_specs=[pl.BlockSpec((tm, d), lambda i: (i, 0)),
                  pl.BlockSpec((d, n), lambda i: (0, 0))],
        out_specs=pl.BlockSpec((tm, n), lambda i: (i, 0)),
        out_shape=jax.ShapeDtypeStruct((t, n), BF16),
        compiler_params=_params("parallel"),
        name="inproj",
    )(x, w)


def _attn_kernel(q_ref, k_ref, v_ref, o_ref, c_ref, acc_ref, *, tq, tk):
    qi = pl.program_id(2)
    q = q_ref[0] * jnp.asarray(1.0 / math.sqrt(ATTN_HEAD_DIM), BF16)
    lane = lax.broadcasted_iota(jnp.int32, (tq, LANES), 1)
    is_h0 = lane < ATTN_HEAD_DIM
    zero = jnp.zeros_like(q)
    q_heads = (jnp.where(is_h0, q, zero), jnp.where(is_h0, zero, q))
    row = lax.broadcasted_iota(jnp.int32, (tq, tk), 0) + qi * tq
    col = lax.broadcasted_iota(jnp.int32, (tq, tk), 1)
    rj = lax.broadcasted_iota(jnp.int32, (tk, tk + LANES), 0)
    cs = lax.broadcasted_iota(jnp.int32, (tk, tk + LANES), 1)
    cum = jnp.where(cs >= tk, 1.0, jnp.where(rj > cs, 1.0, 0.0)).astype(BF16)

    c_ref[...] = jnp.zeros_like(c_ref)
    acc_ref[...] = jnp.zeros_like(acc_ref)

    def body(carry):
        kb, _ = carry
        koff = pl.multiple_of(kb * tk, tk)
        k = k_ref[0, pl.ds(koff, tk), :]
        v = v_ref[0, pl.ds(koff, tk), :]
        mask = (col + kb * tk) < row
        for h in range(2):
            z = lax.dot_general(q_heads[h], k, (((1,), (1,)), ((), ())),
                                preferred_element_type=F32)
            sp = jnp.log1p(jnp.exp(-jnp.abs(z)))
            log_beta = jnp.minimum(z, 0.0) - sp
            l1m = jnp.where(mask, jnp.minimum(-z, 0.0) - sp, 0.0)
            l_hi = l1m.astype(BF16)
            l_lo = (l1m - l_hi.astype(F32)).astype(BF16)
            s = (jnp.dot(l_hi, cum, preferred_element_type=F32)
                 + jnp.dot(l_lo, cum, preferred_element_type=F32))
            c = c_ref[h]
            a = jnp.where(mask, jnp.exp(log_beta + s[:, :tk] + c), 0.0)
            acc_ref[h] += jnp.dot(a.astype(BF16), v, preferred_element_type=F32)
            c_ref[h] = c + s[:, tk:]
        return kb - 1, jnp.max(jnp.maximum(c_ref[0], c_ref[1]))

    def cond(carry):
        kb, cmax = carry
        return jnp.logical_and(kb >= 0, cmax > EXP_ZERO_BELOW)

    kb0 = ((qi + 1) * tq) // tk - 1
    lax.while_loop(cond, body, (kb0, jnp.zeros((), F32)))
    o_ref[0] = jnp.where(is_h0, acc_ref[0], acc_ref[1]).astype(o_ref.dtype)


def _attention(h3):
    b, s, _ = h3.shape
    tq = tk = LANES
    npairs = ATTN_WIDTH // LANES
    return pl.pallas_call(
        functools.partial(_attn_kernel, tq=tq, tk=tk),
        grid=(b, npairs, s // tq),
        in_specs=[pl.BlockSpec((1, tq, LANES), lambda bi, hp, qi: (bi, qi, hp)),
                  pl.BlockSpec((1, s, LANES), lambda bi, hp, qi: (bi, 0, npairs + hp)),
                  pl.BlockSpec((1, s, LANES), lambda bi, hp, qi: (bi, 0, 2 * npairs + hp))],
        out_specs=pl.BlockSpec((1, tq, LANES), lambda bi, hp, qi: (bi, qi, hp)),
        out_shape=jax.ShapeDtypeStruct((b, s, ATTN_WIDTH), BF16),
        scratch_shapes=[pltpu.VMEM((2, tq, LANES), F32), pltpu.VMEM((2, tq, LANES), F32)],
        compiler_params=_params("parallel", "parallel", "arbitrary"),
        name="sb_attention",
    )(h3, h3, h3)


def _gmlp_kernel(gu_ref, gv_ref, vng_ref, vnb_ref, ws_ref, bias_ref, og_ref, o_ref):
    u = _gelu(gu_ref[...].astype(F32))
    v = _gelu(gv_ref[...].astype(F32))
    v = _layer_norm(v, vng_ref[...], vnb_ref[...])
    r = lax.broadcasted_iota(jnp.int32, (GMLP_CHUNK, GMLP_CHUNK), 0)
    c = lax.broadcasted_iota(jnp.int32, (GMLP_CHUNK, GMLP_CHUNK), 1)
    causal = c <= r
    first_group = c < GMLP_GROUP_DIM
    cols = []
    for p in range(GMLP_WIDTH // LANES):
        vp = v[:, p * LANES:(p + 1) * LANES].astype(BF16)
        w0 = jnp.where(causal, ws_ref[2 * p], 0.0).astype(BF16)
        w1 = jnp.where(causal, ws_ref[2 * p + 1], 0.0).astype(BF16)
        m0 = jnp.dot(w0, vp, preferred_element_type=F32)
        m1 = jnp.dot(w1, vp, preferred_element_type=F32)
        cols.append(jnp.where(first_group, m0, m1))
    mixed = jnp.concatenate(cols, axis=1) + bias_ref[...]
    o_ref[...] = _rms_norm(u * mixed, og_ref[...]).astype(o_ref.dtype)


def _gmlp(h, vn_g, vn_b, w_s, b_s, out_g):
    t = h.shape[0]
    u_blk = (3 * ATTN_WIDTH) // GMLP_WIDTH
    bias = jnp.repeat(b_s.T, GMLP_GROUP_DIM, axis=1)
    row = lambda a: a.reshape(1, -1)
    full = lambda shape: pl.BlockSpec(shape, lambda i: (0,) * len(shape))
    return pl.pallas_call(
        _gmlp_kernel,
        grid=(t // GMLP_CHUNK,),
        in_specs=[pl.BlockSpec((GMLP_CHUNK, GMLP_WIDTH), lambda i: (i, u_blk)),
                  pl.BlockSpec((GMLP_CHUNK, GMLP_WIDTH), lambda i: (i, u_blk + 1)),
                  full((1, GMLP_WIDTH)), full((1, GMLP_WIDTH)),
                  full((GMLP_GROUPS, GMLP_CHUNK, GMLP_CHUNK)),
                  full((GMLP_CHUNK, GMLP_WIDTH)), full((1, GMLP_WIDTH))],
        out_specs=pl.BlockSpec((GMLP_CHUNK, GMLP_WIDTH), lambda i: (i, 0)),
        out_shape=jax.ShapeDtypeStruct((t, GMLP_WIDTH), BF16),
        compiler_params=_params("parallel"),
        name="gmlp_gating",
    )(h, h, row(vn_g), row(vn_b), w_s, bias, row(out_g))


def _outproj_kernel(attn_ref, gm_ref, x_ref, ag_ref, w_ref, g_ref, b_ref, x1_ref, x1t_ref, *, alpha):
    a = _rms_norm(attn_ref[...].astype(F32), ag_ref[...]).astype(BF16)
    mix = (jnp.dot(a, w_ref[:ATTN_WIDTH, :], preferred_element_type=F32)
           + jnp.dot(gm_ref[...], w_ref[ATTN_WIDTH:, :], preferred_element_type=F32))
    x1 = _layer_norm(alpha * x_ref[...] + mix, g_ref[...], b_ref[...])
    x1_ref[...] = x1
    x1t_ref[...] = x1.T.astype(BF16)


def _outproj(attn, gm, x, attn_g, w_out, ln_g, ln_b, alpha):
    t, d = x.shape
    tm = min(256, t)
    row = lambda a: a.reshape(1, -1)
    full = lambda shape: pl.BlockSpec(shape, lambda i: (0,) * len(shape))
    return pl.pallas_call(
        functools.partial(_outproj_kernel, alpha=alpha),
        grid=(t // tm,),
        in_specs=[pl.BlockSpec((tm, ATTN_WIDTH), lambda i: (i, 0)),
                  pl.BlockSpec((tm, GMLP_WIDTH), lambda i: (i, 0)),
                  pl.BlockSpec((tm, d), lambda i: (i, 0)),
                  full((1, ATTN_WIDTH)), full(w_out.shape), full((1, d)), full((1, d))],
        out_specs=[pl.BlockSpec((tm, d), lambda i: (i, 0)),
                   pl.BlockSpec((d, tm), lambda i: (0, i))],
        out_shape=[jax.ShapeDtypeStruct((t, d), F32), jax.ShapeDtypeStruct((d, t), BF16)],
        compiler_params=_params("parallel"),
        name="outproj_ln1",
    )(attn, gm, x, row(attn_g), w_out, row(ln_g), row(ln_b))


def _cmp_exchange(x, i, j):
    hi = jnp.maximum(x[i], x[j])
    lo = jnp.minimum(x[i], x[j])
    x[i], x[j] = hi, lo


def _bitonic_merge_desc(x):
    n = len(x)
    d = n // 2
    while d >= 1:
        for i in range(n):
            if i & d == 0:
                _cmp_exchange(x, i, i | d)
        d //= 2


def _bitonic_sort_desc(x):
    n = len(x)
    k = 2
    while k <= n:
        j = k // 2
        while j >= 1:
            for i in range(n):
                l = i ^ j
                if l > i:
                    if i & k == 0:
                        _cmp_exchange(x, i, l)
                    else:
                        _cmp_exchange(x, l, i)
            j //= 2
        k *= 2


def _merge_top(a, b):
    n = len(a)
    c = [jnp.maximum(a[k], b[n - 1 - k]) for k in range(n)]
    _bitonic_merge_desc(c)
    return c


def _peer_kernel(x1t_ref, x1_ref, wqt_ref, keys_ref, u_ref, vt_ref, g_ref, b_ref, o_ref,
                 s_ref, top_ref, hd_ref, e1_ref, n1_ref, e2_ref, rank2_ref,
                 ht_ref, at_ref, acc_ref, *, tm, eb, alpha):
    j = pl.program_id(1)
    nlt = tm // LANES
    nhp = 2 * PEER_HEADS
    nv = PEER_N_KEYS // SUBLANES

    @pl.when(j == 0)
    def _route():
        qt = jnp.dot(wqt_ref[...], x1t_ref[...], preferred_element_type=F32).astype(BF16)
        for hp in range(nhp):
            s_ref[hp] = jnp.dot(keys_ref[hp], qt[hp * PEER_N_KEYS:(hp + 1) * PEER_N_KEYS, :],
                                preferred_element_type=F32)

        def sort_body(hp, carry):
            for lt in range(nlt):
                sl = slice(lt * LANES, (lt + 1) * LANES)
                x = [s_ref[hp, v * SUBLANES:(v + 1) * SUBLANES, sl] for v in range(nv)]
                _bitonic_sort_desc(x)
                for shift in (4, 2, 1):
                    x = _merge_top(x, [pltpu.roll(xi, shift, 0) for xi in x])
                for k in range(PEER_TOPK):
                    top_ref[hp, k:k + 1, sl] = x[k][0:1, :]
            return carry

        lax.fori_loop(0, nhp, sort_body, 0)

        sub = lax.broadcasted_iota(jnp.int32, (SUBLANES, LANES), 0)
        for lt in range(nlt):
            sl = slice(lt * LANES, (lt + 1) * LANES)

            def heads_on_sublanes(p, k):
                out = jnp.zeros((SUBLANES, LANES), F32)
                for h in range(PEER_HEADS):
                    out = jnp.where(sub == h, top_ref[2 * h + p, k:k + 1, sl], out)
                return out

            v1 = [heads_on_sublanes(0, k) for k in range(PEER_TOPK)]
            v2 = [heads_on_sublanes(1, k) for k in range(PEER_TOPK)]
            c = [v1[0] + v2[k] for k in range(PEER_TOPK)]
            for i in range(1, PEER_TOPK):
                c = _merge_top(c, [v1[i] + v2[k] for k in range(PEER_TOPK)])
            z = jnp.zeros((SUBLANES, LANES), F32)
            for k in range(PEER_TOPK):
                z = z + jnp.exp(c[k] - c[0])
            per_head = (c[PEER_TOPK - 1], v1[0], v2[0], 1.0 / z)
            for q, val in enumerate(per_head):
                for h in range(PEER_HEADS):
                    hd_ref[q, h, :, sl] = val[h:h + 1, :]

        rc = PEER_N_KEYS // 2

        def head_body(h, carry):
            for lt in range(nlt):
                sl = slice(lt * LANES, (lt + 1) * LANES)
                tau, m1, m2, rz = (hd_ref[q, h, :, sl] for q in range(4))
                for rows in (slice(0, rc), slice(rc, 2 * rc)):
                    s1 = s_ref[2 * h, rows, sl]
                    s2 = s_ref[2 * h + 1, rows, sl]
                    rank = jnp.zeros((rc, LANES), F32)
                    cnt = jnp.zeros((rc, LANES), F32)
                    for r in range(PEER_TOPK):
                        top2 = top_ref[2 * h + 1, r:r + 1, sl]
                        rank = rank + jnp.where(top2 > s2, 1.0, 0.0)
                        cnt = cnt + jnp.where(s1 + top2 >= tau, 1.0, 0.0)
                    n1_ref[h, rows, sl] = cnt
                    e1_ref[h, rows, sl] = jnp.exp(s1 - m1)
                    rank2_ref[h, rows, sl] = rank.astype(BF16)
                    e2_ref[h, rows, sl] = (jnp.exp(s2 - m2) * rz).astype(BF16)
            return carry

        lax.fori_loop(0, PEER_HEADS, head_body, 0)
        acc_ref[...] = jnp.zeros_like(acc_ref)

    sw = 2 * PEER_N_KEYS
    ns = eb // sw
    a0 = pl.multiple_of(j * (eb // PEER_N_KEYS), SUBLANES)
    x1t = x1t_ref[...]

    def h_stage(s):
        ht_ref[s % 2] = jnp.dot(u_ref[s * sw:(s + 1) * sw, :], x1t, preferred_element_type=F32)

    def acc_stage(s):
        acc_ref[...] += jnp.dot(vt_ref[:, s * sw:(s + 1) * sw], at_ref[s % 2],
                                preferred_element_type=F32)

    nas = sw // PEER_N_KEYS
    zero = jnp.zeros((PEER_N_KEYS, LANES), BF16)

    def gate_stage(s):
        for lt in range(nlt):
            sl = slice(lt * LANES, (lt + 1) * LANES)
            g = [zero for _ in range(nas)]
            for h in range(PEER_HEADS):
                n1t = n1_ref[h, pl.ds(a0, SUBLANES), sl]
                e1t = e1_ref[h, pl.ds(a0, SUBLANES), sl]
                rank2 = rank2_ref[h, :, sl]
                e2 = e2_ref[h, :, sl]
                for ai in range(nas):
                    al = s * nas + ai
                    n1 = jnp.broadcast_to(n1t[al:al + 1, :], (PEER_N_KEYS, LANES)).astype(BF16)
                    e1 = jnp.broadcast_to(e1t[al:al + 1, :], (PEER_N_KEYS, LANES)).astype(BF16)
                    g[ai] = g[ai] + jnp.where(rank2 < n1, e1 * e2, zero)
            for ai in range(nas):
                rows = slice(ai * PEER_N_KEYS, (ai + 1) * PEER_N_KEYS)
                at_ref[s % 2, rows, sl] = g[ai] * _gelu(ht_ref[s % 2, rows, sl]).astype(BF16)

    h_stage(0)
    for s in range(ns):
        if s + 1 < ns:
            h_stage(s + 1)
        if s >= 1:
            acc_stage(s - 1)
        gate_stage(s)
    acc_stage(ns - 1)

    @pl.when(j == pl.num_programs(1) - 1)
    def _finish():
        y = alpha * x1_ref[...] + acc_ref[...].T
        o_ref[...] = _layer_norm(y, g_ref[...], b_ref[...])


def _peer(x1t, x1, wqt, keys, u, vt, ln_g, ln_b, alpha):
    t, d = x1.shape
    e = u.shape[0]
    tm = min(512, t)
    eb = SUBLANES * PEER_N_KEYS
    nhp = 2 * PEER_HEADS
    row = lambda a: a.reshape(1, -1)
    full = lambda shape: pl.BlockSpec(shape, lambda i, j: (0,) * len(shape))
    return pl.pallas_call(
        functools.partial(_peer_kernel, tm=tm, eb=eb, alpha=alpha),
        grid=(t // tm, e // eb),
        in_specs=[pl.BlockSpec((d, tm), lambda i, j: (0, i)),
                  pl.BlockSpec((tm, d), lambda i, j: (i, 0)),
                  full(wqt.shape), full(keys.shape),
                  pl.BlockSpec((eb, d), lambda i, j: (j, 0)),
                  pl.BlockSpec((d, eb), lambda i, j: (0, j)),
                  full((1, d)), full((1, d))],
        out_specs=pl.BlockSpec((tm, d), lambda i, j: (i, 0)),
        out_shape=jax.ShapeDtypeStruct((t, d), F32),
        scratch_shapes=[pltpu.VMEM((nhp, PEER_N_KEYS, tm), F32),
                        pltpu.VMEM((nhp, PEER_TOPK, tm), F32),
                        pltpu.VMEM((4, PEER_HEADS, 1, tm), F32),
                        pltpu.VMEM((PEER_HEADS, PEER_N_KEYS, tm), F32),
                        pltpu.VMEM((PEER_HEADS, PEER_N_KEYS, tm), F32),
                        pltpu.VMEM((PEER_HEADS, PEER_N_KEYS, tm), BF16),
                        pltpu.VMEM((PEER_HEADS, PEER_N_KEYS, tm), BF16),
                        pltpu.VMEM((2, 2 * PEER_N_KEYS, tm), F32),
                        pltpu.VMEM((2, 2 * PEER_N_KEYS, tm), BF16),
                        pltpu.VMEM((d, tm), F32)],
        compiler_params=_params("parallel", "arbitrary"),
        name="peer_ln2",
    )(x1t, x1, wqt, keys, u, vt, row(ln_g), row(ln_b))


def _ple_kernel(x_ref, p_ref, wg_ref, bg_ref, wp_ref, g_ref, b_ref, o_ref, *, alpha):
    x = x_ref[...]
    gate = jax.nn.sigmoid(jnp.dot(x.astype(BF16), wg_ref[...], preferred_element_type=F32)
                          + bg_ref[...])
    ple = gate * jnp.dot(p_ref[...].astype(BF16), wp_ref[...], preferred_element_type=F32)
    o_ref[...] = _layer_norm(alpha * x + ple, g_ref[...], b_ref[...])


def _ple(x, p, w_gate, b_gate, w_proj, ln_g, ln_b, alpha):
    t, d = x.shape
    tm = min(512, t)
    row = lambda a: a.reshape(1, -1)
    full = lambda shape: pl.BlockSpec(shape, lambda i: (0,) * len(shape))
    return pl.pallas_call(
        functools.partial(_ple_kernel, alpha=alpha),
        grid=(t // tm,),
        in_specs=[pl.BlockSpec((tm, d), lambda i: (i, 0)),
                  pl.BlockSpec((tm, p.shape[1]), lambda i: (i, 0)),
                  full(w_gate.shape), full((1, d)), full(w_proj.shape), full((1, d)), full((1, d))],
        out_specs=pl.BlockSpec((tm, d), lambda i: (i, 0)),
        out_shape=jax.ShapeDtypeStruct((t, d), F32),
        compiler_params=_params("parallel"),
        name="ple_ln3",
    )(x, p, w_gate, row(b_gate), w_proj, row(ln_g), row(ln_b))


def kernel(x, p, w_in, attn_out_g, gmlp_vn_g, gmlp_vn_b, gmlp_ws, gmlp_bs, gmlp_out_g, w_out, ln1_g, ln1_b, peer_wq, peer_subkeys, peer_u, peer_v, ln2_g, ln2_b, ple_wproj, ple_wgate, ple_bgate, ln3_g, ln3_b):
    b, s, d = x.shape
    depth = w_in.shape[0]
    t = b * s
    alpha = (2.0 * depth) ** 0.25
    xf = x.reshape(t, d)
    for i in range(depth):
        h = _inproj(xf, w_in[i].astype(BF16))
        attn = _attention(h.reshape(b, s, -1)).reshape(t, ATTN_WIDTH)
        gm = _gmlp(h, gmlp_vn_g[i], gmlp_vn_b[i], gmlp_ws[i], gmlp_bs[i], gmlp_out_g[i])
        x1, x1t = _outproj(attn, gm, xf, attn_out_g[i], w_out[i].astype(BF16), ln1_g[i], ln1_b[i], alpha)
        keys = peer_subkeys[i].reshape(2 * PEER_HEADS, PEER_N_KEYS, -1).astype(BF16)
        x2 = _peer(x1t, x1, peer_wq[i].T.astype(BF16), keys, peer_u[i].astype(BF16),
                   peer_v[i].T.astype(BF16), ln2_g[i], ln2_b[i], alpha)
        xf = _ple(x2, p[i].reshape(t, -1), ple_wgate[i].astype(BF16), ple_bgate[i],
                  ple_wproj[i].astype(BF16), ln3_g[i], ln3_b[i], alpha)
    return xf.reshape(b, s, d)
```

```python
import functools
import math

import jax
import jax.numpy as jnp
from jax import lax
from jax.experimental import pallas as pl
from jax.experimental.pallas import tpu as pltpu

F32 = jnp.float32
BF16 = jnp.bfloat16

ATTN_HEADS = 8
ATTN_HEAD_DIM = 64
ATTN_WIDTH = ATTN_HEADS * ATTN_HEAD_DIM
GMLP_GROUPS = 8
GMLP_GROUP_DIM = 64
GMLP_WIDTH = GMLP_GROUPS * GMLP_GROUP_DIM
GMLP_CHUNK = 128
PEER_HEADS = 8
PEER_N_KEYS = 128
PEER_TOPK = 16
LN_EPS = 1e-5

LANES = 128
SUBLANES = 8
VMEM_LIMIT = 56 * 1024 * 1024

EXP_ZERO_BELOW = -110.0


def _gelu(x):
    return 0.5 * x * (1.0 + lax.erf(x * (1.0 / math.sqrt(2.0))))


def _layer_norm(x, g, b):
    mu = jnp.mean(x, axis=-1, keepdims=True)
    xc = x - mu
    var = jnp.mean(xc * xc, axis=-1, keepdims=True)
    return xc * lax.rsqrt(var + LN_EPS) * g + b


def _rms_norm(x, g):
    return x * lax.rsqrt(jnp.mean(x * x, axis=-1, keepdims=True) + LN_EPS) * g


def _params(*sem):
    return pltpu.CompilerParams(dimension_semantics=sem, vmem_limit_bytes=VMEM_LIMIT)


def _inproj_kernel(x_ref, w_ref, o_ref):
    o_ref[...] = jnp.dot(x_ref[...].astype(BF16), w_ref[...],
                         preferred_element_type=F32).astype(o_ref.dtype)


def _inproj(x, w):
    t, d = x.shape
    n = w.shape[1]
    tm = min(512, t)
    return pl.pallas_call(
        _inproj_kernel,
        grid=(t // tm,),
        in_specs=[pl.BlockSpec((tm, d), lambda i: (i, 0)),
                  pl.BlockSpec((d, n), lambda i: (0, 0))],
        out_specs=pl.BlockSpec((tm, n), lambda i: (i, 0)),
        out_shape=jax.ShapeDtypeStruct((t, n), BF16),
        compiler_params=_params("parallel"),
        name="inproj",
    )(x, w)


def _attn_kernel(q_ref, k_ref, v_ref, o_ref, c_ref, acc_ref, *, tq, tk):
    qi = pl.program_id(2)
    q = q_ref[0] * jnp.asarray(1.0 / math.sqrt(ATTN_HEAD_DIM), BF16)
    lane = lax.broadcasted_iota(jnp.int32, (tq, LANES), 1)
    is_h0 = lane < ATTN_HEAD_DIM
    zero = jnp.zeros_like(q)
    q_heads = (jnp.where(is_h0, q, zero), jnp.where(is_h0, zero, q))
    row = lax.broadcasted_iota(jnp.int32, (tq, tk), 0) + qi * tq
    col = lax.broadcasted_iota(jnp.int32, (tq, tk), 1)
    rj = lax.broadcasted_iota(jnp.int32, (tk, tk + LANES), 0)
    cs = lax.broadcasted_iota(jnp.int32, (tk, tk + LANES), 1)
    cum = jnp.where(cs >= tk, 1.0, jnp.where(rj > cs, 1.0, 0.0)).astype(BF16)

    c_ref[...] = jnp.zeros_like(c_ref)
    acc_ref[...] = jnp.zeros_like(acc_ref)

    def body(carry):
        kb, _ = carry
        koff = pl.multiple_of(kb * tk, tk)
        k = k_ref[0, pl.ds(koff, tk), :]
        v = v_ref[0, pl.ds(koff, tk), :]
        mask = (col + kb * tk) < row
        for h in range(2):
            z = lax.dot_general(q_heads[h], k, (((1,), (1,)), ((), ())),
                                preferred_element_type=F32)
            sp = jnp.log1p(jnp.exp(-jnp.abs(z)))
            log_beta = jnp.minimum(z, 0.0) - sp
            l1m = jnp.where(mask, jnp.minimum(-z, 0.0) - sp, 0.0)
            l_hi = l1m.astype(BF16)
            l_lo = (l1m - l_hi.astype(F32)).astype(BF16)
            s = (jnp.dot(l_hi, cum, preferred_element_type=F32)
                 + jnp.dot(l_lo, cum, preferred_element_type=F32))
            c = c_ref[h]
            a = jnp.where(mask, jnp.exp(log_beta + s[:, :tk] + c), 0.0)
            acc_ref[h] += jnp.dot(a.astype(BF16), v, preferred_element_type=F32)
            c_ref[h] = c + s[:, tk:]
        return kb - 1, jnp.max(jnp.maximum(c_ref[0], c_ref[1]))

    def cond(carry):
        kb, cmax = carry
        return jnp.logical_and(kb >= 0, cmax > EXP_ZERO_BELOW)

    kb0 = ((qi + 1) * tq) // tk - 1
    lax.while_loop(cond, body, (kb0, jnp.zeros((), F32)))
    o_ref[0] = jnp.where(is_h0, acc_ref[0], acc_ref[1]).astype(o_ref.dtype)


def _attention(h3):
    b, s, _ = h3.shape
    tq = tk = LANES
    npairs = ATTN_WIDTH // LANES
    return pl.pallas_call(
        functools.partial(_attn_kernel, tq=tq, tk=tk),
        grid=(b, npairs, s // tq),
        in_specs=[pl.BlockSpec((1, tq, LANES), lambda bi, hp, qi: (bi, qi, hp)),
                  pl.BlockSpec((1, s, LANES), lambda bi, hp, qi: (bi, 0, npairs + hp)),
                  pl.BlockSpec((1, s, LANES), lambda bi, hp, qi: (bi, 0, 2 * npairs + hp))],
        out_specs=pl.BlockSpec((1, tq, LANES), lambda bi, hp, qi: (bi, qi, hp)),
        out_shape=jax.ShapeDtypeStruct((b, s, ATTN_WIDTH), BF16),
        scratch_shapes=[pltpu.VMEM((2, tq, LANES), F32), pltpu.VMEM((2, tq, LANES), F32)],
        compiler_params=_params("parallel", "parallel", "arbitrary"),
        name="sb_attention",
    )(h3, h3, h3)


def _gmlp_kernel(gu_ref, gv_ref, vng_ref, vnb_ref, ws_ref, bias_ref, og_ref, o_ref):
    u = _gelu(gu_ref[...].astype(F32))
    v = _gelu(gv_ref[...].astype(F32))
    v = _layer_norm(v, vng_ref[...], vnb_ref[...])
    r = lax.broadcasted_iota(jnp.int32, (GMLP_CHUNK, GMLP_CHUNK), 0)
    c = lax.broadcasted_iota(jnp.int32, (GMLP_CHUNK, GMLP_CHUNK), 1)
    causal = c <= r
    first_group = c < GMLP_GROUP_DIM
    cols = []
    for p in range(GMLP_WIDTH // LANES):
        vp = v[:, p * LANES:(p + 1) * LANES].astype(BF16)
        w0 = jnp.where(causal, ws_ref[2 * p], 0.0).astype(BF16)
        w1 = jnp.where(causal, ws_ref[2 * p + 1], 0.0).astype(BF16)
        m0 = jnp.dot(w0, vp, preferred_element_type=F32)
        m1 = jnp.dot(w1, vp, preferred_element_type=F32)
        cols.append(jnp.where(first_group, m0, m1))
    mixed = jnp.concatenate(cols, axis=1) + bias_ref[...]
    o_ref[...] = _rms_norm(u * mixed, og_ref[...]).astype(o_ref.dtype)


def _gmlp(h, vn_g, vn_b, w_s, b_s, out_g):
    t = h.shape[0]
    u_blk = (3 * ATTN_WIDTH) // GMLP_WIDTH
    bias = jnp.repeat(b_s.T, GMLP_GROUP_DIM, axis=1)
    row = lambda a: a.reshape(1, -1)
    full = lambda shape: pl.BlockSpec(shape, lambda i: (0,) * len(shape))
    return pl.pallas_call(
        _gmlp_kernel,
        grid=(t // GMLP_CHUNK,),
        in_specs=[pl.BlockSpec((GMLP_CHUNK, GMLP_WIDTH), lambda i: (i, u_blk)),
                  pl.BlockSpec((GMLP_CHUNK, GMLP_WIDTH), lambda i: (i, u_blk + 1)),
                  full((1, GMLP_WIDTH)), full((1, GMLP_WIDTH)),
                  full((GMLP_GROUPS, GMLP_CHUNK, GMLP_CHUNK)),
                  full((GMLP_CHUNK, GMLP_WIDTH)), full((1, GMLP_WIDTH))],
        out_specs=pl.BlockSpec((GMLP_CHUNK, GMLP_WIDTH), lambda i: (i, 0)),
        out_shape=jax.ShapeDtypeStruct((t, GMLP_WIDTH), BF16),
        compiler_params=_params("parallel"),
        name="gmlp_gating",
    )(h, h, row(vn_g), row(vn_b), w_s, bias, row(out_g))


def _outproj_kernel(attn_ref, gm_ref, x_ref, ag_ref, w_ref, g_ref, b_ref, x1_ref, x1t_ref, *, alpha):
    a = _rms_norm(attn_ref[...].astype(F32), ag_ref[...]).astype(BF16)
    mix = (jnp.dot(a, w_ref[:ATTN_WIDTH, :], preferred_element_type=F32)
           + jnp.dot(gm_ref[...], w_ref[ATTN_WIDTH:, :], preferred_element_type=F32))
    x1 = _layer_norm(alpha * x_ref[...] + mix, g_ref[...], b_ref[...])
    x1_ref[...] = x1
    x1t_ref[...] = x1.T.astype(BF16)


def _outproj(attn, gm, x, attn_g, w_out, ln_g, ln_b, alpha):
    t, d = x.shape
    tm = min(256, t)
    row = lambda a: a.reshape(1, -1)
    full = lambda shape: pl.BlockSpec(shape, lambda i: (0,) * len(shape))
    return pl.pallas_call(
        functools.partial(_outproj_kernel, alpha=alpha),
        grid=(t // tm,),
        in_specs=[pl.BlockSpec((tm, ATTN_WIDTH), lambda i: (i, 0)),
                  pl.BlockSpec((tm, GMLP_WIDTH), lambda i: (i, 0)),
                  pl.BlockSpec((tm, d), lambda i: (i, 0)),
                  full((1, ATTN_WIDTH)), full(w_out.shape), full((1, d)), full((1, d))],
        out_specs=[pl.BlockSpec((tm, d), lambda i: (i, 0)),
                   pl.BlockSpec((d, tm), lambda i: (0, i))],
        out_shape=[jax.ShapeDtypeStruct((t, d), F32), jax.ShapeDtypeStruct((d, t), BF16)],
        compiler_params=_params("parallel"),
        name="outproj_ln1",
    )(attn, gm, x, row(attn_g), w_out, row(ln_g), row(ln_b))


def _cmp_exchange(x, i, j):
    hi = jnp.maximum(x[i], x[j])
    lo = jnp.minimum(x[i], x[j])
    x[i], x[j] = hi, lo


def _bitonic_merge_desc(x):
    n = len(x)
    d = n // 2
    while d >= 1:
        for i in range(n):
            if i & d == 0:
                _cmp_exchange(x, i, i | d)
        d //= 2


def _bitonic_sort_desc(x):
    n = len(x)
    k = 2
    while k <= n:
        j = k // 2
        while j >= 1:
            for i in range(n):
                l = i ^ j
                if l > i:
                    if i & k == 0:
                        _cmp_exchange(x, i, l)
                    else:
                        _cmp_exchange(x, l, i)
            j //= 2
        k *= 2


def _merge_top(a, b):
    n = len(a)
    c = [jnp.maximum(a[k], b[n - 1 - k]) for k in range(n)]
    _bitonic_merge_desc(c)
    return c


def _peer_kernel(x1t_ref, x1_ref, wqt_ref, keys_ref, u_ref, vta_ref, vtb_ref, g_ref, b_ref, o_ref,
                 s_ref, top_ref, hd_ref, e1_ref, n1_ref, e2_ref, rank2_ref,
                 ht_ref, at_ref, acc_ref, *, tm, eb, alpha):
    j = pl.program_id(1)
    nlt = tm // LANES
    nhp = 2 * PEER_HEADS
    nv = PEER_N_KEYS // SUBLANES

    @pl.when(j == 0)
    def _route():
        qt = jnp.dot(wqt_ref[...], x1t_ref[...], preferred_element_type=F32).astype(BF16)
        for hp in range(nhp):
            s_ref[hp] = jnp.dot(keys_ref[hp], qt[hp * PEER_N_KEYS:(hp + 1) * PEER_N_KEYS, :],
                                preferred_element_type=F32)

        def sort_body(hp, carry):
            for lt in range(nlt):
                sl = slice(lt * LANES, (lt + 1) * LANES)
                x = [s_ref[hp, v * SUBLANES:(v + 1) * SUBLANES, sl] for v in range(nv)]
                _bitonic_sort_desc(x)
                for shift in (4, 2, 1):
                    x = _merge_top(x, [pltpu.roll(xi, shift, 0) for xi in x])
                for k in range(PEER_TOPK):
                    top_ref[hp, k:k + 1, sl] = x[k][0:1, :]
            return carry

        lax.fori_loop(0, nhp, sort_body, 0)

        sub = lax.broadcasted_iota(jnp.int32, (SUBLANES, LANES), 0)
        for lt in range(nlt):
            sl = slice(lt * LANES, (lt + 1) * LANES)

            def heads_on_sublanes(p, k):
                out = jnp.zeros((SUBLANES, LANES), F32)
                for h in range(PEER_HEADS):
                    out = jnp.where(sub == h, top_ref[2 * h + p, k:k + 1, sl], out)
                return out

            v1 = [heads_on_sublanes(0, k) for k in range(PEER_TOPK)]
            v2 = [heads_on_sublanes(1, k) for k in range(PEER_TOPK)]
            c = [v1[0] + v2[k] for k in range(PEER_TOPK)]
            for i in range(1, PEER_TOPK):
                c = _merge_top(c, [v1[i] + v2[k] for k in range(PEER_TOPK)])
            z = jnp.zeros((SUBLANES, LANES), F32)
            for k in range(PEER_TOPK):
                z = z + jnp.exp(c[k] - c[0])
            per_head = (c[PEER_TOPK - 1], v1[0], v2[0], 1.0 / z)
            for q, val in enumerate(per_head):
                for h in range(PEER_HEADS):
                    hd_ref[q, h, :, sl] = val[h:h + 1, :]

        rc = PEER_N_KEYS // 2

        def head_body(h, carry):
            for lt in range(nlt):
                sl = slice(lt * LANES, (lt + 1) * LANES)
                tau, m1, m2, rz = (hd_ref[q, h, :, sl] for q in range(4))
                for rows in (slice(0, rc), slice(rc, 2 * rc)):
                    s1 = s_ref[2 * h, rows, sl]
                    s2 = s_ref[2 * h + 1, rows, sl]
                    rank = jnp.zeros((rc, LANES), F32)
                    cnt = jnp.zeros((rc, LANES), F32)
                    for r in range(PEER_TOPK):
                        top2 = top_ref[2 * h + 1, r:r + 1, sl]
                        rank = rank + jnp.where(top2 > s2, 1.0, 0.0)
                        cnt = cnt + jnp.where(s1 + top2 >= tau, 1.0, 0.0)
                    n1_ref[h, rows, sl] = cnt
                    e1_ref[h, rows, sl] = jnp.exp(s1 - m1)
                    rank2_ref[h, rows, sl] = rank.astype(BF16)
                    e2_ref[h, rows, sl] = (jnp.exp(s2 - m2) * rz).astype(BF16)
            return carry

        lax.fori_loop(0, PEER_HEADS, head_body, 0)
        acc_ref[...] = jnp.zeros_like(acc_ref)
        ht_ref[1] = jnp.zeros(ht_ref.shape[1:], F32)

    nb = 2 * (pl.num_programs(1) - 1)
    ga = 2
    zero = jnp.zeros((PEER_N_KEYS, LANES), BF16)

    def sub_step(cur, vt_blk_ref):
        prev = 1 - cur
        ht_ref[cur] = jnp.dot(u_ref[cur * eb:(cur + 1) * eb, :], x1t_ref[...],
                              preferred_element_type=F32)
        gate(cur)
        acc_ref[...] += jnp.dot(vt_blk_ref[...], at_ref[prev], preferred_element_type=F32)

    def gate(cur):
        prev = 1 - cur
        blk = 2 * j + cur - 1
        exists = jnp.logical_and(blk >= 0, blk < nb).astype(F32)
        a0 = pl.multiple_of(jnp.clip(blk, 0, nb - 1) * SUBLANES, SUBLANES)
        for ag in range(eb // PEER_N_KEYS // ga):
            for lt in range(nlt):
                sl = slice(lt * LANES, (lt + 1) * LANES)
                g = [zero for _ in range(ga)]
                for h in range(PEER_HEADS):
                    n1t = n1_ref[h, pl.ds(a0, SUBLANES), sl] * exists
                    e1t = e1_ref[h, pl.ds(a0, SUBLANES), sl]
                    rank2 = rank2_ref[h, :, sl]
                    e2 = e2_ref[h, :, sl]
                    for ai in range(ga):
                        al = ag * ga + ai
                        n1 = jnp.broadcast_to(n1t[al:al + 1, :], (PEER_N_KEYS, LANES)).astype(BF16)
                        e1 = jnp.broadcast_to(e1t[al:al + 1, :], (PEER_N_KEYS, LANES)).astype(BF16)
                        g[ai] = g[ai] + jnp.where(rank2 < n1, e1 * e2, zero)
                for ai in range(ga):
                    al = ag * ga + ai
                    rows = slice(al * PEER_N_KEYS, (al + 1) * PEER_N_KEYS)
                    at_ref[prev, rows, sl] = g[ai] * _gelu(ht_ref[prev, rows, sl]).astype(BF16)

    sub_step(0, vta_ref)
    sub_step(1, vtb_ref)

    @pl.when(j == pl.num_programs(1) - 1)
    def _finish():
        y = alpha * x1_ref[...] + acc_ref[...].T
        o_ref[...] = _layer_norm(y, g_ref[...], b_ref[...])


def _peer(x1t, x1, wqt, keys, u, vt, ln_g, ln_b, alpha):
    t, d = x1.shape
    e = u.shape[0]
    tm = min(512, t)
    eb = SUBLANES * PEER_N_KEYS
    nhp = 2 * PEER_HEADS
    npair = e // (2 * eb)
    nb = 2 * npair
    row = lambda a: a.reshape(1, -1)
    once = pl.Buffered(1)
    full = lambda shape: pl.BlockSpec(shape, lambda i, j: (0,) * len(shape), pipeline_mode=once)
    return pl.pallas_call(
        functools.partial(_peer_kernel, tm=tm, eb=eb, alpha=alpha),
        grid=(t // tm, npair + 1),
        in_specs=[pl.BlockSpec((d, tm), lambda i, j: (0, i), pipeline_mode=once),
                  pl.BlockSpec((tm, d), lambda i, j: (i, 0), pipeline_mode=once),
                  full(wqt.shape), full(keys.shape),
                  pl.BlockSpec((2 * eb, d), lambda i, j: (jnp.minimum(j, npair - 1), 0)),
                  pl.BlockSpec((d, eb), lambda i, j: (0, jnp.clip(2 * j - 1, 0, nb - 1))),
                  pl.BlockSpec((d, eb), lambda i, j: (0, jnp.clip(2 * j, 0, nb - 1))),
                  full((1, d)), full((1, d))],
        out_specs=pl.BlockSpec((tm, d), lambda i, j: (i, 0)),
        out_shape=jax.ShapeDtypeStruct((t, d), F32),
        scratch_shapes=[pltpu.VMEM((nhp, PEER_N_KEYS, tm), F32),
                        pltpu.VMEM((nhp, PEER_TOPK, tm), F32),
                        pltpu.VMEM((4, PEER_HEADS, 1, tm), F32),
                        pltpu.VMEM((PEER_HEADS, PEER_N_KEYS, tm), F32),
                        pltpu.VMEM((PEER_HEADS, PEER_N_KEYS, tm), F32),
                        pltpu.VMEM((PEER_HEADS, PEER_N_KEYS, tm), BF16),
                        pltpu.VMEM((PEER_HEADS, PEER_N_KEYS, tm), BF16),
                        pltpu.VMEM((2, eb, tm), F32),
                        pltpu.VMEM((2, eb, tm), BF16),
                        pltpu.VMEM((d, tm), F32)],
        compiler_params=_params("parallel", "arbitrary"),
        name="peer_ln2",
    )(x1t, x1, wqt, keys, u, vt, vt, row(ln_g), row(ln_b))


def _ple_kernel(x_ref, p_ref, wg_ref, bg_ref, wp_ref, g_ref, b_ref, o_ref, *, alpha):
    x = x_ref[...]
    gate = jax.nn.sigmoid(jnp.dot(x.astype(BF16), wg_ref[...], preferred_element_type=F32)
                          + bg_ref[...])
    ple = gate * jnp.dot(p_ref[...].astype(BF16), wp_ref[...], preferred_element_type=F32)
    o_ref[...] = _layer_norm(alpha * x + ple, g_ref[...], b_ref[...])


def _ple(x, p, w_gate, b_gate, w_proj, ln_g, ln_b, alpha):
    t, d = x.shape
    tm = min(512, t)
    row = lambda a: a.reshape(1, -1)
    full = lambda shape: pl.BlockSpec(shape, lambda i: (0,) * len(shape))
    return pl.pallas_call(
        functools.partial(_ple_kernel, alpha=alpha),
        grid=(t // tm,),
        in_specs=[pl.BlockSpec((tm, d), lambda i: (i, 0)),
                  pl.BlockSpec((tm, p.shape[1]), lambda i: (i, 0)),
                  full(w_gate.shape), full((1, d)), full(w_proj.shape), full((1, d)), full((1, d))],
        out_specs=pl.BlockSpec((tm, d), lambda i: (i, 0)),
        out_shape=jax.ShapeDtypeStruct((t, d), F32),
        compiler_params=_params("parallel"),
        name="ple_ln3",
    )(x, p, w_gate, row(b_gate), w_proj, row(ln_g), row(ln_b))


def kernel(x, p, w_in, attn_out_g, gmlp_vn_g, gmlp_vn_b, gmlp_ws, gmlp_bs, gmlp_out_g, w_out, ln1_g, ln1_b, peer_wq, peer_subkeys, peer_u, peer_v, ln2_g, ln2_b, ple_wproj, ple_wgate, ple_bgate, ln3_g, ln3_b):
    b, s, d = x.shape
    depth = w_in.shape[0]
    t = b * s
    alpha = (2.0 * depth) ** 0.25
    xf = x.reshape(t, d)
    for i in range(depth):
        h = _inproj(xf, w_in[i].astype(BF16))
        attn = _attention(h.reshape(b, s, -1)).reshape(t, ATTN_WIDTH)
        gm = _gmlp(h, gmlp_vn_g[i], gmlp_vn_b[i], gmlp_ws[i], gmlp_bs[i], gmlp_out_g[i])
        x1, x1t = _outproj(attn, gm, xf, attn_out_g[i], w_out[i].astype(BF16), ln1_g[i], ln1_b[i], alpha)
        keys = peer_subkeys[i].reshape(2 * PEER_HEADS, PEER_N_KEYS, -1).astype(BF16)
        x2 = _peer(x1t, x1, peer_wq[i].T.astype(BF16), keys, peer_u[i].astype(BF16),
                   peer_v[i].T.astype(BF16), ln2_g[i], ln2_b[i], alpha)
        xf = _ple(x2, p[i].reshape(t, -1), ple_wgate[i].astype(BF16), ple_bgate[i],
                  ple_wproj[i].astype(BF16), ln3_g[i], ln3_b[i], alpha)
    return xf.reshape(b, s, d)
```

```python
import functools
import math

import jax
import jax.numpy as jnp
from jax import lax
from jax.experimental import pallas as pl
from jax.experimental.pallas import tpu as pltpu

F32 = jnp.float32
BF16 = jnp.bfloat16

ATTN_HEADS = 8
ATTN_HEAD_DIM = 64
ATTN_WIDTH = ATTN_HEADS * ATTN_HEAD_DIM
GMLP_GROUPS = 8
GMLP_GROUP_DIM = 64
GMLP_WIDTH = GMLP_GROUPS * GMLP_GROUP_DIM
GMLP_CHUNK = 128
PEER_HEADS = 8
PEER_N_KEYS = 128
PEER_TOPK = 16
LN_EPS = 1e-5

LANES = 128
SUBLANES = 8
VMEM_LIMIT = 56 * 1024 * 1024

EXP_ZERO_BELOW = -110.0


def _gelu(x):
    return 0.5 * x * (1.0 + lax.erf(x * (1.0 / math.sqrt(2.0))))


def _layer_norm(x, g, b):
    mu = jnp.mean(x, axis=-1, keepdims=True)
    xc = x - mu
    var = jnp.mean(xc * xc, axis=-1, keepdims=True)
    return xc * lax.rsqrt(var + LN_EPS) * g + b


def _rms_norm(x, g):
    return x * lax.rsqrt(jnp.mean(x * x, axis=-1, keepdims=True) + LN_EPS) * g


def _params(*sem):
    return pltpu.CompilerParams(dimension_semantics=sem, vmem_limit_bytes=VMEM_LIMIT)


def _inproj_kernel(x_ref, w_ref, o_ref):
    o_ref[...] = jnp.dot(x_ref[...].astype(BF16), w_ref[...],
                         preferred_element_type=F32).astype(o_ref.dtype)


def _inproj(x, w):
    t, d = x.shape
    n = w.shape[1]
    tm = min(512, t)
    return pl.pallas_call(
        _inproj_kernel,
        grid=(t // tm,),
        in_specs=[pl.BlockSpec((tm, d), lambda i: (i, 0)),
                  pl.BlockSpec((d, n), lambda i: (0, 0))],
        out_specs=pl.BlockSpec((tm, n), lambda i: (i, 0)),
        out_shape=jax.ShapeDtypeStruct((t, n), BF16),
        compiler_params=_params("parallel"),
        name="inproj",
    )(x, w)


def _attn_kernel(q_ref, k_ref, v_ref, o_ref, c_ref, acc_ref, *, tq, tk, nkb):
    qi = pl.program_id(2)
    q = q_ref[0] * jnp.asarray(1.0 / math.sqrt(ATTN_HEAD_DIM), BF16)
    lane = lax.broadcasted_iota(jnp.int32, (tq, LANES), 1)
    is_h0 = lane < ATTN_HEAD_DIM
    zero = jnp.zeros_like(q)
    q_heads = (jnp.where(is_h0, q, zero), jnp.where(is_h0, zero, q))
    row = lax.broadcasted_iota(jnp.int32, (tq, tk), 0) + qi * tq
    col = lax.broadcasted_iota(jnp.int32, (tq, tk), 1)
    rj = lax.broadcasted_iota(jnp.int32, (tk, tk + LANES), 0)
    cs = lax.broadcasted_iota(jnp.int32, (tk, tk + LANES), 1)
    cum = jnp.where(cs >= tk, 1.0, jnp.where(rj > cs, 1.0, 0.0)).astype(BF16)

    c_ref[...] = jnp.zeros_like(c_ref)
    acc_ref[...] = jnp.zeros_like(acc_ref)
    slab = nkb * tk

    def body(carry):
        hi, _ = carry
        lo = pl.multiple_of(jnp.maximum(hi - slab, 0), tk)
        bound = jnp.minimum(row, hi) - lo
        koffs = [pl.multiple_of(lo + kb * tk, tk) for kb in range(nkb)]
        masks = [(col + kb * tk) < bound for kb in range(nkb)]
        cmax = None
        for h in range(2):
            log_beta, sums = [], []
            for kb in range(nkb):
                k = k_ref[0, pl.ds(koffs[kb], tk), :]
                z = lax.dot_general(q_heads[h], k, (((1,), (1,)), ((), ())),
                                    preferred_element_type=F32)
                sp = jnp.log1p(jnp.exp(-jnp.abs(z)))
                log_beta.append(jnp.minimum(z, 0.0) - sp)
                l1m = jnp.where(masks[kb], jnp.minimum(-z, 0.0) - sp, 0.0)
                l_hi = l1m.astype(BF16)
                l_lo = (l1m - l_hi.astype(F32)).astype(BF16)
                sums.append(jnp.dot(l_hi, cum, preferred_element_type=F32)
                            + jnp.dot(l_lo, cum, preferred_element_type=F32))
            run = c_ref[h]
            pv = jnp.zeros((tq, LANES), F32)
            for kb in reversed(range(nkb)):
                v = v_ref[0, pl.ds(koffs[kb], tk), :]
                a = jnp.where(masks[kb], jnp.exp(log_beta[kb] + sums[kb][:, :tk] + run), 0.0)
                pv = pv + jnp.dot(a.astype(BF16), v, preferred_element_type=F32)
                run = run + sums[kb][:, tk:]
            acc_ref[h] += pv
            c_ref[h] = run
            cmax = run if cmax is None else jnp.maximum(cmax, run)
        return lo, jnp.max(cmax)

    def cond(carry):
        hi, cmax = carry
        return jnp.logical_and(hi > 0, cmax > EXP_ZERO_BELOW)

    lax.while_loop(cond, body, ((qi + 1) * tq, jnp.zeros((), F32)))
    o_ref[0] = jnp.where(is_h0, acc_ref[0], acc_ref[1]).astype(o_ref.dtype)


def _attention(h3):
    b, s, _ = h3.shape
    tq, tk, nkb = 2 * LANES, LANES, 4
    npairs = ATTN_WIDTH // LANES
    return pl.pallas_call(
        functools.partial(_attn_kernel, tq=tq, tk=tk, nkb=nkb),
        grid=(b, npairs, s // tq),
        in_specs=[pl.BlockSpec((1, tq, LANES), lambda bi, hp, qi: (bi, qi, hp)),
                  pl.BlockSpec((1, s, LANES), lambda bi, hp, qi: (bi, 0, npairs + hp)),
                  pl.BlockSpec((1, s, LANES), lambda bi, hp, qi: (bi, 0, 2 * npairs + hp))],
        out_specs=pl.BlockSpec((1, tq, LANES), lambda bi, hp, qi: (bi, qi, hp)),
        out_shape=jax.ShapeDtypeStruct((b, s, ATTN_WIDTH), BF16),
        scratch_shapes=[pltpu.VMEM((2, tq, LANES), F32), pltpu.VMEM((2, tq, LANES), F32)],
        compiler_params=_params("parallel", "parallel", "arbitrary"),
        name="sb_attention",
    )(h3, h3, h3)


def _gmlp_kernel(gu_ref, gv_ref, vng_ref, vnb_ref, ws_ref, bias_ref, og_ref, o_ref):
    u = _gelu(gu_ref[...].astype(F32))
    v = _gelu(gv_ref[...].astype(F32))
    v = _layer_norm(v, vng_ref[...], vnb_ref[...])
    r = lax.broadcasted_iota(jnp.int32, (GMLP_CHUNK, GMLP_CHUNK), 0)
    c = lax.broadcasted_iota(jnp.int32, (GMLP_CHUNK, GMLP_CHUNK), 1)
    causal = c <= r
    first_group = c < GMLP_GROUP_DIM
    cols = []
    for p in range(GMLP_WIDTH // LANES):
        vp = v[:, p * LANES:(p + 1) * LANES].astype(BF16)
        w0 = jnp.where(causal, ws_ref[2 * p], 0.0).astype(BF16)
        w1 = jnp.where(causal, ws_ref[2 * p + 1], 0.0).astype(BF16)
        m0 = jnp.dot(w0, vp, preferred_element_type=F32)
        m1 = jnp.dot(w1, vp, preferred_element_type=F32)
        cols.append(jnp.where(first_group, m0, m1))
    mixed = jnp.concatenate(cols, axis=1) + bias_ref[...]
    o_ref[...] = _rms_norm(u * mixed, og_ref[...]).astype(o_ref.dtype)


def _gmlp(h, vn_g, vn_b, w_s, b_s, out_g):
    t = h.shape[0]
    u_blk = (3 * ATTN_WIDTH) // GMLP_WIDTH
    bias = jnp.repeat(b_s.T, GMLP_GROUP_DIM, axis=1)
    row = lambda a: a.reshape(1, -1)
    full = lambda shape: pl.BlockSpec(shape, lambda i: (0,) * len(shape))
    return pl.pallas_call(
        _gmlp_kernel,
        grid=(t // GMLP_CHUNK,),
        in_specs=[pl.BlockSpec((GMLP_CHUNK, GMLP_WIDTH), lambda i: (i, u_blk)),
                  pl.BlockSpec((GMLP_CHUNK, GMLP_WIDTH), lambda i: (i, u_blk + 1)),
                  full((1, GMLP_WIDTH)), full((1, GMLP_WIDTH)),
                  full((GMLP_GROUPS, GMLP_CHUNK, GMLP_CHUNK)),
                  full((GMLP_CHUNK, GMLP_WIDTH)), full((1, GMLP_WIDTH))],
        out_specs=pl.BlockSpec((GMLP_CHUNK, GMLP_WIDTH), lambda i: (i, 0)),
        out_shape=jax.ShapeDtypeStruct((t, GMLP_WIDTH), BF16),
        compiler_params=_params("parallel"),
        name="gmlp_gating",
    )(h, h, row(vn_g), row(vn_b), w_s, bias, row(out_g))


def _outproj_kernel(attn_ref, gm_ref, x_ref, ag_ref, w_ref, g_ref, b_ref, x1_ref, x1t_ref, *, alpha):
    a = _rms_norm(attn_ref[...].astype(F32), ag_ref[...]).astype(BF16)
    mix = (jnp.dot(a, w_ref[:ATTN_WIDTH, :], preferred_element_type=F32)
           + jnp.dot(gm_ref[...], w_ref[ATTN_WIDTH:, :], preferred_element_type=F32))
    x1 = _layer_norm(alpha * x_ref[...] + mix, g_ref[...], b_ref[...])
    x1_ref[...] = x1
    x1t_ref[...] = x1.T.astype(BF16)


def _outproj(attn, gm, x, attn_g, w_out, ln_g, ln_b, alpha):
    t, d = x.shape
    tm = min(256, t)
    row = lambda a: a.reshape(1, -1)
    full = lambda shape: pl.BlockSpec(shape, lambda i: (0,) * len(shape))
    return pl.pallas_call(
        functools.partial(_outproj_kernel, alpha=alpha),
        grid=(t // tm,),
        in_specs=[pl.BlockSpec((tm, ATTN_WIDTH), lambda i: (i, 0)),
                  pl.BlockSpec((tm, GMLP_WIDTH), lambda i: (i, 0)),
                  pl.BlockSpec((tm, d), lambda i: (i, 0)),
                  full((1, ATTN_WIDTH)), full(w_out.shape), full((1, d)), full((1, d))],
        out_specs=[pl.BlockSpec((tm, d), lambda i: (i, 0)),
                   pl.BlockSpec((d, tm), lambda i: (0, i))],
        out_shape=[jax.ShapeDtypeStruct((t, d), F32), jax.ShapeDtypeStruct((d, t), BF16)],
        compiler_params=_params("parallel"),
        name="outproj_ln1",
    )(attn, gm, x, row(attn_g), w_out, row(ln_g), row(ln_b))


def _cmp_exchange(x, i, j):
    hi = jnp.maximum(x[i], x[j])
    lo = jnp.minimum(x[i], x[j])
    x[i], x[j] = hi, lo


def _bitonic_merge_desc(x):
    n = len(x)
    d = n // 2
    while d >= 1:
        for i in range(n):
            if i & d == 0:
                _cmp_exchange(x, i, i | d)
        d //= 2


def _bitonic_sort_desc(x):
    n = len(x)
    k = 2
    while k <= n:
        j = k // 2
        while j >= 1:
            for i in range(n):
                l = i ^ j
                if l > i:
                    if i & k == 0:
                        _cmp_exchange(x, i, l)
                    else:
                        _cmp_exchange(x, l, i)
            j //= 2
        k *= 2


def _merge_top(a, b):
    n = len(a)
    c = [jnp.maximum(a[k], b[n - 1 - k]) for k in range(n)]
    _bitonic_merge_desc(c)
    return c


def _peer_kernel(x1t_ref, x1_ref, wqt_ref, keys_ref, u_ref, vta_ref, vtb_ref, g_ref, b_ref, o_ref,
                 s_ref, top_ref, hd_ref, e1_ref, n1_ref, e2_ref, rank2_ref,
                 ht_ref, at_ref, acc_ref, *, tm, eb, alpha):
    j = pl.program_id(1)
    nlt = tm // LANES
    nhp = 2 * PEER_HEADS
    nv = PEER_N_KEYS // SUBLANES

    @pl.when(j == 0)
    def _route():
        qt = jnp.dot(wqt_ref[...], x1t_ref[...], preferred_element_type=F32).astype(BF16)
        for hp in range(nhp):
            s_ref[hp] = jnp.dot(keys_ref[hp], qt[hp * PEER_N_KEYS:(hp + 1) * PEER_N_KEYS, :],
                                preferred_element_type=F32)

        def sort_body(hp, carry):
            for lt in range(nlt):
                sl = slice(lt * LANES, (lt + 1) * LANES)
                x = [s_ref[hp, v * SUBLANES:(v + 1) * SUBLANES, sl] for v in range(nv)]
                _bitonic_sort_desc(x)
                for shift in (4, 2, 1):
                    x = _merge_top(x, [pltpu.roll(xi, shift, 0) for xi in x])
                for k in range(PEER_TOPK):
                    top_ref[hp, k:k + 1, sl] = x[k][0:1, :]
            return carry

        lax.fori_loop(0, nhp, sort_body, 0)

        sub = lax.broadcasted_iota(jnp.int32, (SUBLANES, LANES), 0)
        for lt in range(nlt):
            sl = slice(lt * LANES, (lt + 1) * LANES)

            def heads_on_sublanes(p, k):
                out = jnp.zeros((SUBLANES, LANES), F32)
                for h in range(PEER_HEADS):
                    out = jnp.where(sub == h, top_ref[2 * h + p, k:k + 1, sl], out)
                return out

            v1 = [heads_on_sublanes(0, k) for k in range(PEER_TOPK)]
            v2 = [heads_on_sublanes(1, k) for k in range(PEER_TOPK)]
            c = [v1[0] + v2[k] for k in range(PEER_TOPK)]
            for i in range(1, PEER_TOPK):
                c = _merge_top(c, [v1[i] + v2[k] for k in range(PEER_TOPK)])
            z = jnp.zeros((SUBLANES, LANES), F32)
            for k in range(PEER_TOPK):
                z = z + jnp.exp(c[k] - c[0])
            per_head = (c[PEER_TOPK - 1], v1[0], v2[0], 1.0 / z)
            for q, val in enumerate(per_head):
                for h in range(PEER_HEADS):
                    hd_ref[q, h, :, sl] = val[h:h + 1, :]

        rc = PEER_N_KEYS // 2

        def head_body(h, carry):
            for lt in range(nlt):
                sl = slice(lt * LANES, (lt + 1) * LANES)
                tau, m1, m2, rz = (hd_ref[q, h, :, sl] for q in range(4))
                for rows in (slice(0, rc), slice(rc, 2 * rc)):
                    s1 = s_ref[2 * h, rows, sl]
                    s2 = s_ref[2 * h + 1, rows, sl]
                    rank = jnp.zeros((rc, LANES), F32)
                    cnt = jnp.zeros((rc, LANES), F32)
                    for r in range(PEER_TOPK):
                        top2 = top_ref[2 * h + 1, r:r + 1, sl]
                        rank = rank + jnp.where(top2 > s2, 1.0, 0.0)
                        cnt = cnt + jnp.where(s1 + top2 >= tau, 1.0, 0.0)
                    n1_ref[h, rows, sl] = cnt
                    e1_ref[h, rows, sl] = jnp.exp(s1 - m1)
                    rank2_ref[h, rows, sl] = rank.astype(BF16)
                    e2_ref[h, rows, sl] = (jnp.exp(s2 - m2) * rz).astype(BF16)
            return carry

        lax.fori_loop(0, PEER_HEADS, head_body, 0)
        acc_ref[...] = jnp.zeros_like(acc_ref)
        ht_ref[1] = jnp.zeros(ht_ref.shape[1:], F32)

    nb = 2 * (pl.num_programs(1) - 1)
    ga = 2
    zero = jnp.zeros((PEER_N_KEYS, LANES), BF16)

    def sub_step(cur, vt_blk_ref):
        prev = 1 - cur
        ht_ref[cur] = jnp.dot(u_ref[cur * eb:(cur + 1) * eb, :], x1t_ref[...],
                              preferred_element_type=F32)
        gate(cur)
        acc_ref[...] += jnp.dot(vt_blk_ref[...], at_ref[prev], preferred_element_type=F32)

    def gate(cur):
        prev = 1 - cur
        blk = 2 * j + cur - 1
        exists = jnp.logical_and(blk >= 0, blk < nb).astype(F32)
        a0 = pl.multiple_of(jnp.clip(blk, 0, nb - 1) * SUBLANES, SUBLANES)
        for ag in range(eb // PEER_N_KEYS // ga):
            for lt in range(nlt):
                sl = slice(lt * LANES, (lt + 1) * LANES)
                g = [zero for _ in range(ga)]
                for h in range(PEER_HEADS):
                    n1t = n1_ref[h, pl.ds(a0, SUBLANES), sl] * exists
                    e1t = e1_ref[h, pl.ds(a0, SUBLANES), sl]
                    rank2 = rank2_ref[h, :, sl]
                    e2 = e2_ref[h, :, sl]
                    for ai in range(ga):
                        al = ag * ga + ai
                        n1 = jnp.broadcast_to(n1t[al:al + 1, :], (PEER_N_KEYS, LANES)).astype(BF16)
                        e1 = jnp.broadcast_to(e1t[al:al + 1, :], (PEER_N_KEYS, LANES)).astype(BF16)
                        g[ai] = g[ai] + jnp.where(rank2 < n1, e1 * e2, zero)
                for ai in range(ga):
                    al = ag * ga + ai
                    rows = slice(al * PEER_N_KEYS, (al + 1) * PEER_N_KEYS)
                    at_ref[prev, rows, sl] = g[ai] * _gelu(ht_ref[prev, rows, sl]).astype(BF16)

    sub_step(0, vta_ref)
    sub_step(1, vtb_ref)

    @pl.when(j == pl.num_programs(1) - 1)
    def _finish():
        y = alpha * x1_ref[...] + acc_ref[...].T
        o_ref[...] = _layer_norm(y, g_ref[...], b_ref[...])


def _peer(x1t, x1, wqt, keys, u, vt, ln_g, ln_b, alpha):
    t, d = x1.shape
    e = u.shape[0]
    tm = min(512, t)
    eb = SUBLANES * PEER_N_KEYS
    nhp = 2 * PEER_HEADS
    npair = e // (2 * eb)
    nb = 2 * npair
    row = lambda a: a.reshape(1, -1)
    once = pl.Buffered(1)
    full = lambda shape: pl.BlockSpec(shape, lambda i, j: (0,) * len(shape), pipeline_mode=once)
    return pl.pallas_call(
        functools.partial(_peer_kernel, tm=tm, eb=eb, alpha=alpha),
        grid=(t // tm, npair + 1),
        in_specs=[pl.BlockSpec((d, tm), lambda i, j: (0, i), pipeline_mode=once),
                  pl.BlockSpec((tm, d), lambda i, j: (i, 0), pipeline_mode=once),
                  full(wqt.shape), full(keys.shape),
                  pl.BlockSpec((2 * eb, d), lambda i, j: (jnp.minimum(j, npair - 1), 0)),
                  pl.BlockSpec((d, eb), lambda i, j: (0, jnp.clip(2 * j - 1, 0, nb - 1))),
                  pl.BlockSpec((d, eb), lambda i, j: (0, jnp.clip(2 * j, 0, nb - 1))),
                  full((1, d)), full((1, d))],
        out_specs=pl.BlockSpec((tm, d), lambda i, j: (i, 0)),
        out_shape=jax.ShapeDtypeStruct((t, d), F32),
        scratch_shapes=[pltpu.VMEM((nhp, PEER_N_KEYS, tm), F32),
                        pltpu.VMEM((nhp, PEER_TOPK, tm), F32),
                        pltpu.VMEM((4, PEER_HEADS, 1, tm), F32),
                        pltpu.VMEM((PEER_HEADS, PEER_N_KEYS, tm), F32),
                        pltpu.VMEM((PEER_HEADS, PEER_N_KEYS, tm), F32),
                        pltpu.VMEM((PEER_HEADS, PEER_N_KEYS, tm), BF16),
                        pltpu.VMEM((PEER_HEADS, PEER_N_KEYS, tm), BF16),
                        pltpu.VMEM((2, eb, tm), F32),
                        pltpu.VMEM((2, eb, tm), BF16),
                        pltpu.VMEM((d, tm), F32)],
        compiler_params=_params("parallel", "arbitrary"),
        name="peer_ln2",
    )(x1t, x1, wqt, keys, u, vt, vt, row(ln_g), row(ln_b))


def _ple_kernel(x_ref, p_ref, wg_ref, bg_ref, wp_ref, g_ref, b_ref, o_ref, *, alpha):
    x = x_ref[...]
    gate = jax.nn.sigmoid(jnp.dot(x.astype(BF16), wg_ref[...], preferred_element_type=F32)
                          + bg_ref[...])
    ple = gate * jnp.dot(p_ref[...].astype(BF16), wp_ref[...], preferred_element_type=F32)
    o_ref[...] = _layer_norm(alpha * x + ple, g_ref[...], b_ref[...])


def _ple(x, p, w_gate, b_gate, w_proj, ln_g, ln_b, alpha):
    t, d = x.shape
    tm = min(512, t)
    row = lambda a: a.reshape(1, -1)
    full = lambda shape: pl.BlockSpec(shape, lambda i: (0,) * len(shape))
    return pl.pallas_call(
        functools.partial(_ple_kernel, alpha=alpha),
        grid=(t // tm,),
        in_specs=[pl.BlockSpec((tm, d), lambda i: (i, 0)),
                  pl.BlockSpec((tm, p.shape[1]), lambda i: (i, 0)),
                  full(w_gate.shape), full((1, d)), full(w_proj.shape), full((1, d)), full((1, d))],
        out_specs=pl.BlockSpec((tm, d), lambda i: (i, 0)),
        out_shape=jax.ShapeDtypeStruct((t, d), F32),
        compiler_params=_params("parallel"),
        name="ple_ln3",
    )(x, p, w_gate, row(b_gate), w_proj, row(ln_g), row(ln_b))


def kernel(x, p, w_in, attn_out_g, gmlp_vn_g, gmlp_vn_b, gmlp_ws, gmlp_bs, gmlp_out_g, w_out, ln1_g, ln1_b, peer_wq, peer_subkeys, peer_u, peer_v, ln2_g, ln2_b, ple_wproj, ple_wgate, ple_bgate, ln3_g, ln3_b):
    b, s, d = x.shape
    depth = w_in.shape[0]
    t = b * s
    alpha = (2.0 * depth) ** 0.25
    xf = x.reshape(t, d)
    for i in range(depth):
        h = _inproj(xf, w_in[i].astype(BF16))
        attn = _attention(h.reshape(b, s, -1)).reshape(t, ATTN_WIDTH)
        gm = _gmlp(h, gmlp_vn_g[i], gmlp_vn_b[i], gmlp_ws[i], gmlp_bs[i], gmlp_out_g[i])
        x1, x1t = _outproj(attn, gm, xf, attn_out_g[i], w_out[i].astype(BF16), ln1_g[i], ln1_b[i], alpha)
        keys = peer_subkeys[i].reshape(2 * PEER_HEADS, PEER_N_KEYS, -1).astype(BF16)
        x2 = _peer(x1t, x1, peer_wq[i].T.astype(BF16), keys, peer_u[i].astype(BF16),
                   peer_v[i].T.astype(BF16), ln2_g[i], ln2_b[i], alpha)
        xf = _ple(x2, p[i].reshape(t, -1), ple_wgate[i].astype(BF16), ple_bgate[i],
                  ple_wproj[i].astype(BF16), ln3_g[i], ln3_b[i], alpha)
    return xf.reshape(b, s, d)
```

```python
import functools
import math

import jax
import jax.numpy as jnp
from jax import lax
from jax.experimental import pallas as pl
from jax.experimental.pallas import tpu as pltpu

F32 = jnp.float32
BF16 = jnp.bfloat16

ATTN_HEADS = 8
ATTN_HEAD_DIM = 64
ATTN_WIDTH = ATTN_HEADS * ATTN_HEAD_DIM
GMLP_GROUPS = 8
GMLP_GROUP_DIM = 64
GMLP_WIDTH = GMLP_GROUPS * GMLP_GROUP_DIM
GMLP_CHUNK = 128
PEER_HEADS = 8
PEER_N_KEYS = 128
PEER_TOPK = 16
LN_EPS = 1e-5

LANES = 128
SUBLANES = 8
VMEM_LIMIT = 58 * 1024 * 1024

EXP_ZERO_BELOW = -110.0


def _gelu(x):
    return 0.5 * x * (1.0 + lax.erf(x * (1.0 / math.sqrt(2.0))))


def _layer_norm(x, g, b):
    mu = jnp.mean(x, axis=-1, keepdims=True)
    xc = x - mu
    var = jnp.mean(xc * xc, axis=-1, keepdims=True)
    return xc * lax.rsqrt(var + LN_EPS) * g + b


def _rms_norm(x, g):
    return x * lax.rsqrt(jnp.mean(x * x, axis=-1, keepdims=True) + LN_EPS) * g


def _params(*sem):
    return pltpu.CompilerParams(dimension_semantics=sem, vmem_limit_bytes=VMEM_LIMIT)


def _inproj_kernel(x_ref, w_ref, o_ref):
    o_ref[...] = jnp.dot(x_ref[...].astype(BF16), w_ref[...],
                         preferred_element_type=F32).astype(o_ref.dtype)


def _inproj(x, w):
    t, d = x.shape
    n = w.shape[1]
    tm = min(512, t)
    return pl.pallas_call(
        _inproj_kernel,
        grid=(t // tm,),
        in_specs=[pl.BlockSpec((tm, d), lambda i: (i, 0)),
                  pl.BlockSpec((d, n), lambda i: (0, 0))],
        out_specs=pl.BlockSpec((tm, n), lambda i: (i, 0)),
        out_shape=jax.ShapeDtypeStruct((t, n), BF16),
        compiler_params=_params("parallel"),
        name="inproj",
    )(x, w)


def _attn_kernel(q_ref, k_ref, v_ref, o_ref, c_ref, acc_ref, *, tq, tk, nkb):
    qi = pl.program_id(2)
    q = q_ref[0] * jnp.asarray(1.0 / math.sqrt(ATTN_HEAD_DIM), BF16)
    lane = lax.broadcasted_iota(jnp.int32, (tq, LANES), 1)
    is_h0 = lane < ATTN_HEAD_DIM
    zero = jnp.zeros_like(q)
    q_heads = (jnp.where(is_h0, q, zero), jnp.where(is_h0, zero, q))
    row = lax.broadcasted_iota(jnp.int32, (tq, tk), 0) + qi * tq
    col = lax.broadcasted_iota(jnp.int32, (tq, tk), 1)
    rj = lax.broadcasted_iota(jnp.int32, (tk, tk + LANES), 0)
    cs = lax.broadcasted_iota(jnp.int32, (tk, tk + LANES), 1)
    cum = jnp.where(cs >= tk, 1.0, jnp.where(rj > cs, 1.0, 0.0)).astype(BF16)

    c_ref[...] = jnp.zeros_like(c_ref)
    acc_ref[...] = jnp.zeros_like(acc_ref)
    slab = nkb * tk

    def body(carry):
        hi, _ = carry
        lo = pl.multiple_of(jnp.maximum(hi - slab, 0), tk)
        bound = jnp.minimum(row, hi) - lo
        koffs = [pl.multiple_of(lo + kb * tk, tk) for kb in range(nkb)]
        masks = [(col + kb * tk) < bound for kb in range(nkb)]
        cmax = None
        for h in range(2):
            log_beta, sums = [], []
            for kb in range(nkb):
                k = k_ref[0, pl.ds(koffs[kb], tk), :]
                z = lax.dot_general(q_heads[h], k, (((1,), (1,)), ((), ())),
                                    preferred_element_type=F32)
                sp = jnp.log1p(jnp.exp(-jnp.abs(z)))
                log_beta.append(jnp.minimum(z, 0.0) - sp)
                l1m = jnp.where(masks[kb], jnp.minimum(-z, 0.0) - sp, 0.0)
                l_hi = l1m.astype(BF16)
                l_lo = (l1m - l_hi.astype(F32)).astype(BF16)
                sums.append(jnp.dot(l_hi, cum, preferred_element_type=F32)
                            + jnp.dot(l_lo, cum, preferred_element_type=F32))
            run = c_ref[h]
            pv = jnp.zeros((tq, LANES), F32)
            for kb in reversed(range(nkb)):
                v = v_ref[0, pl.ds(koffs[kb], tk), :]
                a = jnp.where(masks[kb], jnp.exp(log_beta[kb] + sums[kb][:, :tk] + run), 0.0)
                pv = pv + jnp.dot(a.astype(BF16), v, preferred_element_type=F32)
                run = run + sums[kb][:, tk:]
            acc_ref[h] += pv
            c_ref[h] = run
            cmax = run if cmax is None else jnp.maximum(cmax, run)
        return lo, jnp.max(cmax)

    def cond(carry):
        hi, cmax = carry
        return jnp.logical_and(hi > 0, cmax > EXP_ZERO_BELOW)

    lax.while_loop(cond, body, ((qi + 1) * tq, jnp.zeros((), F32)))
    o_ref[0] = jnp.where(is_h0, acc_ref[0], acc_ref[1]).astype(o_ref.dtype)


def _attention(h3):
    b, s, _ = h3.shape
    tq, tk, nkb = 2 * LANES, LANES, 4
    npairs = ATTN_WIDTH // LANES
    return pl.pallas_call(
        functools.partial(_attn_kernel, tq=tq, tk=tk, nkb=nkb),
        grid=(b, npairs, s // tq),
        in_specs=[pl.BlockSpec((1, tq, LANES), lambda bi, hp, qi: (bi, qi, hp)),
                  pl.BlockSpec((1, s, LANES), lambda bi, hp, qi: (bi, 0, npairs + hp)),
                  pl.BlockSpec((1, s, LANES), lambda bi, hp, qi: (bi, 0, 2 * npairs + hp))],
        out_specs=pl.BlockSpec((1, tq, LANES), lambda bi, hp, qi: (bi, qi, hp)),
        out_shape=jax.ShapeDtypeStruct((b, s, ATTN_WIDTH), BF16),
        scratch_shapes=[pltpu.VMEM((2, tq, LANES), F32), pltpu.VMEM((2, tq, LANES), F32)],
        compiler_params=_params("parallel", "parallel", "arbitrary"),
        name="sb_attention",
    )(h3, h3, h3)


def _gmlp_kernel(gu_ref, gv_ref, vng_ref, vnb_ref, ws_ref, bias_ref, og_ref, o_ref):
    u = _gelu(gu_ref[...].astype(F32))
    v = _gelu(gv_ref[...].astype(F32))
    v = _layer_norm(v, vng_ref[...], vnb_ref[...])
    r = lax.broadcasted_iota(jnp.int32, (GMLP_CHUNK, GMLP_CHUNK), 0)
    c = lax.broadcasted_iota(jnp.int32, (GMLP_CHUNK, GMLP_CHUNK), 1)
    causal = c <= r
    first_group = c < GMLP_GROUP_DIM
    cols = []
    for p in range(GMLP_WIDTH // LANES):
        vp = v[:, p * LANES:(p + 1) * LANES].astype(BF16)
        w0 = jnp.where(causal, ws_ref[2 * p], 0.0).astype(BF16)
        w1 = jnp.where(causal, ws_ref[2 * p + 1], 0.0).astype(BF16)
        m0 = jnp.dot(w0, vp, preferred_element_type=F32)
        m1 = jnp.dot(w1, vp, preferred_element_type=F32)
        cols.append(jnp.where(first_group, m0, m1))
    mixed = jnp.concatenate(cols, axis=1) + bias_ref[...]
    o_ref[...] = _rms_norm(u * mixed, og_ref[...]).astype(o_ref.dtype)


def _gmlp(h, vn_g, vn_b, w_s, b_s, out_g):
    t = h.shape[0]
    u_blk = (3 * ATTN_WIDTH) // GMLP_WIDTH
    bias = jnp.repeat(b_s.T, GMLP_GROUP_DIM, axis=1)
    row = lambda a: a.reshape(1, -1)
    full = lambda shape: pl.BlockSpec(shape, lambda i: (0,) * len(shape))
    return pl.pallas_call(
        _gmlp_kernel,
        grid=(t // GMLP_CHUNK,),
        in_specs=[pl.BlockSpec((GMLP_CHUNK, GMLP_WIDTH), lambda i: (i, u_blk)),
                  pl.BlockSpec((GMLP_CHUNK, GMLP_WIDTH), lambda i: (i, u_blk + 1)),
                  full((1, GMLP_WIDTH)), full((1, GMLP_WIDTH)),
                  full((GMLP_GROUPS, GMLP_CHUNK, GMLP_CHUNK)),
                  full((GMLP_CHUNK, GMLP_WIDTH)), full((1, GMLP_WIDTH))],
        out_specs=pl.BlockSpec((GMLP_CHUNK, GMLP_WIDTH), lambda i: (i, 0)),
        out_shape=jax.ShapeDtypeStruct((t, GMLP_WIDTH), BF16),
        compiler_params=_params("parallel"),
        name="gmlp_gating",
    )(h, h, row(vn_g), row(vn_b), w_s, bias, row(out_g))


def _outproj_kernel(attn_ref, gm_ref, x_ref, ag_ref, w_ref, g_ref, b_ref, x1_ref, x1t_ref, *, alpha):
    a = _rms_norm(attn_ref[...].astype(F32), ag_ref[...]).astype(BF16)
    mix = (jnp.dot(a, w_ref[:ATTN_WIDTH, :], preferred_element_type=F32)
           + jnp.dot(gm_ref[...], w_ref[ATTN_WIDTH:, :], preferred_element_type=F32))
    x1 = _layer_norm(alpha * x_ref[...] + mix, g_ref[...], b_ref[...])
    x1_ref[...] = x1
    x1t_ref[...] = x1.T.astype(BF16)


def _outproj(attn, gm, x, attn_g, w_out, ln_g, ln_b, alpha):
    t, d = x.shape
    tm = min(256, t)
    row = lambda a: a.reshape(1, -1)
    full = lambda shape: pl.BlockSpec(shape, lambda i: (0,) * len(shape))
    return pl.pallas_call(
        functools.partial(_outproj_kernel, alpha=alpha),
        grid=(t // tm,),
        in_specs=[pl.BlockSpec((tm, ATTN_WIDTH), lambda i: (i, 0)),
                  pl.BlockSpec((tm, GMLP_WIDTH), lambda i: (i, 0)),
                  pl.BlockSpec((tm, d), lambda i: (i, 0)),
                  full((1, ATTN_WIDTH)), full(w_out.shape), full((1, d)), full((1, d))],
        out_specs=[pl.BlockSpec((tm, d), lambda i: (i, 0)),
                   pl.BlockSpec((d, tm), lambda i: (0, i))],
        out_shape=[jax.ShapeDtypeStruct((t, d), F32), jax.ShapeDtypeStruct((d, t), BF16)],
        compiler_params=_params("parallel"),
        name="outproj_ln1",
    )(attn, gm, x, row(attn_g), w_out, row(ln_g), row(ln_b))


def _cmp_exchange(x, i, j):
    hi = jnp.maximum(x[i], x[j])
    lo = jnp.minimum(x[i], x[j])
    x[i], x[j] = hi, lo


def _bitonic_merge_desc(x):
    n = len(x)
    d = n // 2
    while d >= 1:
        for i in range(n):
            if i & d == 0:
                _cmp_exchange(x, i, i | d)
        d //= 2


def _bitonic_sort_desc(x):
    n = len(x)
    k = 2
    while k <= n:
        j = k // 2
        while j >= 1:
            for i in range(n):
                l = i ^ j
                if l > i:
                    if i & k == 0:
                        _cmp_exchange(x, i, l)
                    else:
                        _cmp_exchange(x, l, i)
            j //= 2
        k *= 2


def _merge_top(a, b):
    n = len(a)
    c = [jnp.maximum(a[k], b[n - 1 - k]) for k in range(n)]
    _bitonic_merge_desc(c)
    return c


def _peer_kernel(x1t_ref, x1tb_ref, x1_ref, wqt_ref, keys_ref, u0_ref, ua_ref, ub_ref,
                 vta_ref, vtb_ref, g_ref, b_ref, o_ref,
                 s_ref, top_ref, hd_ref, e1_ref, n1_ref, e2_ref, rank2_ref,
                 ht_ref, at_ref, acc_ref, *, tm, eb, alpha):
    ti = pl.program_id(0)
    j = pl.program_id(1)
    nlt = tm // LANES
    nhp = 2 * PEER_HEADS
    nv = PEER_N_KEYS // SUBLANES

    @pl.when(j == 0)
    def _route():
        qt = jnp.dot(wqt_ref[...], x1t_ref[...], preferred_element_type=F32).astype(BF16)
        for hp in range(nhp):
            s_ref[hp] = jnp.dot(keys_ref[hp], qt[hp * PEER_N_KEYS:(hp + 1) * PEER_N_KEYS, :],
                                preferred_element_type=F32)

        def sort_body(hp, carry):
            for lt in range(nlt):
                sl = slice(lt * LANES, (lt + 1) * LANES)
                x = [s_ref[hp, v * SUBLANES:(v + 1) * SUBLANES, sl] for v in range(nv)]
                _bitonic_sort_desc(x)
                for shift in (4, 2, 1):
                    x = _merge_top(x, [pltpu.roll(xi, shift, 0) for xi in x])
                for k in range(PEER_TOPK):
                    top_ref[hp, k:k + 1, sl] = x[k][0:1, :]
            return carry

        lax.fori_loop(0, nhp, sort_body, 0)

        sub = lax.broadcasted_iota(jnp.int32, (SUBLANES, LANES), 0)
        for lt in range(nlt):
            sl = slice(lt * LANES, (lt + 1) * LANES)

            def heads_on_sublanes(p, k):
                out = jnp.zeros((SUBLANES, LANES), F32)
                for h in range(PEER_HEADS):
                    out = jnp.where(sub == h, top_ref[2 * h + p, k:k + 1, sl], out)
                return out

            v1 = [heads_on_sublanes(0, k) for k in range(PEER_TOPK)]
            v2 = [heads_on_sublanes(1, k) for k in range(PEER_TOPK)]
            c = [v1[0] + v2[k] for k in range(PEER_TOPK)]
            for r in range(1, PEER_TOPK):
                c = _merge_top(c, [v1[r] + v2[k] for k in range(PEER_TOPK)])
            z = jnp.zeros((SUBLANES, LANES), F32)
            for k in range(PEER_TOPK):
                z = z + jnp.exp(c[k] - c[0])
            per_head = (c[PEER_TOPK - 1], v1[0], v2[0], 1.0 / z)
            for q, val in enumerate(per_head):
                for h in range(PEER_HEADS):
                    hd_ref[q, h, :, sl] = val[h:h + 1, :]

        rc = PEER_N_KEYS // 2

        def head_body(h, carry):
            for lt in range(nlt):
                sl = slice(lt * LANES, (lt + 1) * LANES)
                tau, m1, m2, rz = (hd_ref[q, h, :, sl] for q in range(4))
                for rows in (slice(0, rc), slice(rc, 2 * rc)):
                    s1 = s_ref[2 * h, rows, sl]
                    s2 = s_ref[2 * h + 1, rows, sl]
                    rank = jnp.zeros((rc, LANES), F32)
                    cnt = jnp.zeros((rc, LANES), F32)
                    for r in range(PEER_TOPK):
                        top2 = top_ref[2 * h + 1, r:r + 1, sl]
                        rank = rank + jnp.where(top2 > s2, 1.0, 0.0)
                        cnt = cnt + jnp.where(s1 + top2 >= tau, 1.0, 0.0)
                    n1_ref[h, rows, sl] = cnt
                    e1_ref[h, rows, sl] = jnp.exp(s1 - m1)
                    rank2_ref[h, rows, sl] = rank.astype(BF16)
                    e2_ref[h, rows, sl] = (jnp.exp(s2 - m2) * rz).astype(BF16)
            return carry

        lax.fori_loop(0, PEER_HEADS, head_body, 0)
        acc_ref[...] = jnp.zeros_like(acc_ref)

        @pl.when(ti == 0)
        def _first_block():
            ht_ref[0] = jnp.dot(u0_ref[...], x1t_ref[...], preferred_element_type=F32)

    ga = 2
    zero = jnp.zeros((PEER_N_KEYS, LANES), BF16)

    def sub_step(par, u_next_ref, xt_next_ref, vt_blk_ref):
        ht_ref[1 - par] = jnp.dot(u_next_ref[...], xt_next_ref[...], preferred_element_type=F32)
        gate(par)
        acc_ref[...] += jnp.dot(vt_blk_ref[...], at_ref[par], preferred_element_type=F32)

    def gate(par):
        a0 = pl.multiple_of((2 * j + par) * SUBLANES, SUBLANES)
        for ag in range(eb // PEER_N_KEYS // ga):
            for lt in range(nlt):
                sl = slice(lt * LANES, (lt + 1) * LANES)
                g = [zero for _ in range(ga)]
                for h in range(PEER_HEADS):
                    n1t = n1_ref[h, pl.ds(a0, SUBLANES), sl]
                    e1t = e1_ref[h, pl.ds(a0, SUBLANES), sl]
                    rank2 = rank2_ref[h, :, sl]
                    e2 = e2_ref[h, :, sl]
                    for ai in range(ga):
                        al = ag * ga + ai
                        n1 = jnp.broadcast_to(n1t[al:al + 1, :], (PEER_N_KEYS, LANES)).astype(BF16)
                        e1 = jnp.broadcast_to(e1t[al:al + 1, :], (PEER_N_KEYS, LANES)).astype(BF16)
                        g[ai] = g[ai] + jnp.where(rank2 < n1, e1 * e2, zero)
                for ai in range(ga):
                    al = ag * ga + ai
                    rows = slice(al * PEER_N_KEYS, (al + 1) * PEER_N_KEYS)
                    at_ref[par, rows, sl] = g[ai] * _gelu(ht_ref[par, rows, sl]).astype(BF16)

    sub_step(0, ua_ref, x1t_ref, vta_ref)
    sub_step(1, ub_ref, x1tb_ref, vtb_ref)

    @pl.when(j == pl.num_programs(1) - 1)
    def _finish():
        y = alpha * x1_ref[...] + acc_ref[...].T
        o_ref[...] = _layer_norm(y, g_ref[...], b_ref[...])


def _peer(x1t, x1, wqt, keys, u, vt, ln_g, ln_b, alpha):
    t, d = x1.shape
    e = u.shape[0]
    tm = min(512, t)
    eb = SUBLANES * PEER_N_KEYS
    nhp = 2 * PEER_HEADS
    nb = e // eb
    nt = t // tm
    row = lambda a: a.reshape(1, -1)
    once = pl.Buffered(1)
    full = lambda shape: pl.BlockSpec(shape, lambda i, j: (0,) * len(shape), pipeline_mode=once)
    next_tile = lambda i, j: jnp.where(j == nb // 2 - 1, jnp.minimum(i + 1, nt - 1), i)
    return pl.pallas_call(
        functools.partial(_peer_kernel, tm=tm, eb=eb, alpha=alpha),
        grid=(nt, nb // 2),
        in_specs=[pl.BlockSpec((d, tm), lambda i, j: (0, i), pipeline_mode=once),
                  pl.BlockSpec((d, tm), lambda i, j: (0, next_tile(i, j))),
                  pl.BlockSpec((tm, d), lambda i, j: (i, 0), pipeline_mode=once),
                  full(wqt.shape), full(keys.shape),
                  full((eb, d)),
                  pl.BlockSpec((eb, d), lambda i, j: (2 * j + 1, 0)),
                  pl.BlockSpec((eb, d), lambda i, j: ((2 * j + 2) % nb, 0)),
                  pl.BlockSpec((d, eb), lambda i, j: (0, 2 * j)),
                  pl.BlockSpec((d, eb), lambda i, j: (0, 2 * j + 1)),
                  full((1, d)), full((1, d))],
        out_specs=pl.BlockSpec((tm, d), lambda i, j: (i, 0)),
        out_shape=jax.ShapeDtypeStruct((t, d), F32),
        scratch_shapes=[pltpu.VMEM((nhp, PEER_N_KEYS, tm), F32),
                        pltpu.VMEM((nhp, PEER_TOPK, tm), F32),
                        pltpu.VMEM((4, PEER_HEADS, 1, tm), F32),
                        pltpu.VMEM((PEER_HEADS, PEER_N_KEYS, tm), F32),
                        pltpu.VMEM((PEER_HEADS, PEER_N_KEYS, tm), F32),
                        pltpu.VMEM((PEER_HEADS, PEER_N_KEYS, tm), BF16),
                        pltpu.VMEM((PEER_HEADS, PEER_N_KEYS, tm), BF16),
                        pltpu.VMEM((2, eb, tm), F32),
                        pltpu.VMEM((2, eb, tm), BF16),
                        pltpu.VMEM((d, tm), F32)],
        compiler_params=_params("arbitrary", "arbitrary"),
        name="peer_ln2",
    )(x1t, x1t, x1, wqt, keys, u, u, u, vt, vt, row(ln_g), row(ln_b))


def _ple_kernel(x_ref, p_ref, wg_ref, bg_ref, wp_ref, g_ref, b_ref, o_ref, *, alpha):
    x = x_ref[...]
    gate = jax.nn.sigmoid(jnp.dot(x.astype(BF16), wg_ref[...], preferred_element_type=F32)
                          + bg_ref[...])
    ple = gate * jnp.dot(p_ref[...].astype(BF16), wp_ref[...], preferred_element_type=F32)
    o_ref[...] = _layer_norm(alpha * x + ple, g_ref[...], b_ref[...])


def _ple(x, p, w_gate, b_gate, w_proj, ln_g, ln_b, alpha):
    t, d = x.shape
    tm = min(512, t)
    row = lambda a: a.reshape(1, -1)
    full = lambda shape: pl.BlockSpec(shape, lambda i: (0,) * len(shape))
    return pl.pallas_call(
        functools.partial(_ple_kernel, alpha=alpha),
        grid=(t // tm,),
        in_specs=[pl.BlockSpec((tm, d), lambda i: (i, 0)),
                  pl.BlockSpec((tm, p.shape[1]), lambda i: (i, 0)),
                  full(w_gate.shape), full((1, d)), full(w_proj.shape), full((1, d)), full((1, d))],
        out_specs=pl.BlockSpec((tm, d), lambda i: (i, 0)),
        out_shape=jax.ShapeDtypeStruct((t, d), F32),
        compiler_params=_params("parallel"),
        name="ple_ln3",
    )(x, p, w_gate, row(b_gate), w_proj, row(ln_g), row(ln_b))


def kernel(x, p, w_in, attn_out_g, gmlp_vn_g, gmlp_vn_b, gmlp_ws, gmlp_bs, gmlp_out_g, w_out, ln1_g, ln1_b, peer_wq, peer_subkeys, peer_u, peer_v, ln2_g, ln2_b, ple_wproj, ple_wgate, ple_bgate, ln3_g, ln3_b):
    b, s, d = x.shape
    depth = w_in.shape[0]
    t = b * s
    alpha = (2.0 * depth) ** 0.25
    xf = x.reshape(t, d)
    for i in range(depth):
        h = _inproj(xf, w_in[i].astype(BF16))
        attn = _attention(h.reshape(b, s, -1)).reshape(t, ATTN_WIDTH)
        gm = _gmlp(h, gmlp_vn_g[i], gmlp_vn_b[i], gmlp_ws[i], gmlp_bs[i], gmlp_out_g[i])
        x1, x1t = _outproj(attn, gm, xf, attn_out_g[i], w_out[i].astype(BF16), ln1_g[i], ln1_b[i], alpha)
        keys = peer_subkeys[i].reshape(2 * PEER_HEADS, PEER_N_KEYS, -1).astype(BF16)
        x2 = _peer(x1t, x1, peer_wq[i].T.astype(BF16), keys, peer_u[i].astype(BF16),
                   peer_v[i].T.astype(BF16), ln2_g[i], ln2_b[i], alpha)
        xf = _ple(x2, p[i].reshape(t, -1), ple_wgate[i].astype(BF16), ple_bgate[i],
                  ple_wproj[i].astype(BF16), ln3_g[i], ln3_b[i], alpha)
    return xf.reshape(b, s, d)
```

```python
import functools
import math

import jax
import jax.numpy as jnp
from jax import lax
from jax.experimental import pallas as pl
from jax.experimental.pallas import tpu as pltpu

F32 = jnp.float32
BF16 = jnp.bfloat16

ATTN_HEADS = 8
ATTN_HEAD_DIM = 64
ATTN_WIDTH = ATTN_HEADS * ATTN_HEAD_DIM
GMLP_GROUPS = 8
GMLP_GROUP_DIM = 64
GMLP_WIDTH = GMLP_GROUPS * GMLP_GROUP_DIM
GMLP_CHUNK = 128
PEER_HEADS = 8
PEER_N_KEYS = 128
PEER_TOPK = 16
LN_EPS = 1e-5

LANES = 128
SUBLANES = 8
VMEM_LIMIT = 58 * 1024 * 1024

EXP_ZERO_BELOW = -110.0


def _gelu(x):
    return 0.5 * x * (1.0 + lax.erf(x * (1.0 / math.sqrt(2.0))))


def _layer_norm(x, g, b):
    mu = jnp.mean(x, axis=-1, keepdims=True)
    xc = x - mu
    var = jnp.mean(xc * xc, axis=-1, keepdims=True)
    return xc * lax.rsqrt(var + LN_EPS) * g + b


def _rms_norm(x, g):
    return x * lax.rsqrt(jnp.mean(x * x, axis=-1, keepdims=True) + LN_EPS) * g


def _params(*sem):
    return pltpu.CompilerParams(dimension_semantics=sem, vmem_limit_bytes=VMEM_LIMIT)


def _inproj_kernel(x_ref, w_ref, o_ref):
    o_ref[...] = jnp.dot(x_ref[...].astype(BF16), w_ref[...],
                         preferred_element_type=F32).astype(o_ref.dtype)


def _inproj(x, w):
    t, d = x.shape
    n = w.shape[1]
    tm = min(512, t)
    return pl.pallas_call(
        _inproj_kernel,
        grid=(t // tm,),
        in_specs=[pl.BlockSpec((tm, d), lambda i: (i, 0)),
                  pl.BlockSpec((d, n), lambda i: (0, 0))],
        out_specs=pl.BlockSpec((tm, n), lambda i: (i, 0)),
        out_shape=jax.ShapeDtypeStruct((t, n), BF16),
        compiler_params=_params("parallel"),
        name="inproj",
    )(x, w)


def _attn_kernel(q_ref, k_ref, v_ref, o_ref, c_ref, acc_ref, *, tq, tk, nkb):
    qi = pl.program_id(2)
    q = q_ref[0] * jnp.asarray(1.0 / math.sqrt(ATTN_HEAD_DIM), BF16)
    lane = lax.broadcasted_iota(jnp.int32, (tq, LANES), 1)
    is_h0 = lane < ATTN_HEAD_DIM
    zero = jnp.zeros_like(q)
    q_heads = (jnp.where(is_h0, q, zero), jnp.where(is_h0, zero, q))
    row = lax.broadcasted_iota(jnp.int32, (tq, tk), 0) + qi * tq
    col = lax.broadcasted_iota(jnp.int32, (tq, tk), 1)
    rj = lax.broadcasted_iota(jnp.int32, (tk, tk + LANES), 0)
    cs = lax.broadcasted_iota(jnp.int32, (tk, tk + LANES), 1)
    cum = jnp.where(cs >= tk, 1.0, jnp.where(rj > cs, 1.0, 0.0)).astype(BF16)
    cum2 = jnp.concatenate([cum, cum], axis=0)

    c_ref[...] = jnp.zeros_like(c_ref)
    acc_ref[...] = jnp.zeros_like(acc_ref)
    slab = nkb * tk

    def body(carry):
        hi, _ = carry
        lo = pl.multiple_of(jnp.maximum(hi - slab, 0), tk)
        bound = jnp.minimum(row, hi) - lo
        koffs = [pl.multiple_of(lo + kb * tk, tk) for kb in range(nkb)]
        masks = [(col + kb * tk) < bound for kb in range(nkb)]
        cmax = None
        for h in range(2):
            log_beta, sums = [], []
            for kb in range(nkb):
                k = k_ref[0, pl.ds(koffs[kb], tk), :]
                z = lax.dot_general(q_heads[h], k, (((1,), (1,)), ((), ())),
                                    preferred_element_type=F32)
                sp = jnp.log(1.0 + jnp.exp(-jnp.abs(z)))
                log_beta.append(jnp.minimum(z, 0.0) - sp)
                l1m = jnp.where(masks[kb], jnp.minimum(-z, 0.0) - sp, 0.0)
                l_hi = l1m.astype(BF16)
                l_lo = (l1m - l_hi.astype(F32)).astype(BF16)
                sums.append(jnp.dot(jnp.concatenate([l_hi, l_lo], axis=1), cum2,
                                    preferred_element_type=F32))
            run = c_ref[h]
            pv = jnp.zeros((tq, LANES), F32)
            for kb in reversed(range(nkb)):
                v = v_ref[0, pl.ds(koffs[kb], tk), :]
                a = jnp.where(masks[kb], jnp.exp(log_beta[kb] + sums[kb][:, :tk] + run), 0.0)
                pv = pv + jnp.dot(a.astype(BF16), v, preferred_element_type=F32)
                run = run + sums[kb][:, tk:]
            acc_ref[h] += pv
            c_ref[h] = run
            cmax = run if cmax is None else jnp.maximum(cmax, run)
        return lo, jnp.max(cmax)

    def cond(carry):
        hi, cmax = carry
        return jnp.logical_and(hi > 0, cmax > EXP_ZERO_BELOW)

    lax.while_loop(cond, body, ((qi + 1) * tq, jnp.zeros((), F32)))
    o_ref[0] = jnp.where(is_h0, acc_ref[0], acc_ref[1]).astype(o_ref.dtype)


def _attention(h3):
    b, s, _ = h3.shape
    tq, tk, nkb = 2 * LANES, LANES, 4
    npairs = ATTN_WIDTH // LANES
    return pl.pallas_call(
        functools.partial(_attn_kernel, tq=tq, tk=tk, nkb=nkb),
        grid=(b, npairs, s // tq),
        in_specs=[pl.BlockSpec((1, tq, LANES), lambda bi, hp, qi: (bi, qi, hp)),
                  pl.BlockSpec((1, s, LANES), lambda bi, hp, qi: (bi, 0, npairs + hp)),
                  pl.BlockSpec((1, s, LANES), lambda bi, hp, qi: (bi, 0, 2 * npairs + hp))],
        out_specs=pl.BlockSpec((1, tq, LANES), lambda bi, hp, qi: (bi, qi, hp)),
        out_shape=jax.ShapeDtypeStruct((b, s, ATTN_WIDTH), BF16),
        scratch_shapes=[pltpu.VMEM((2, tq, LANES), F32), pltpu.VMEM((2, tq, LANES), F32)],
        compiler_params=_params("parallel", "parallel", "arbitrary"),
        name="sb_attention",
    )(h3, h3, h3)


def _gmlp_kernel(gu_ref, gv_ref, vng_ref, vnb_ref, ws_ref, bias_ref, og_ref, o_ref):
    u = _gelu(gu_ref[...].astype(F32))
    v = _gelu(gv_ref[...].astype(F32))
    v = _layer_norm(v, vng_ref[...], vnb_ref[...])
    r = lax.broadcasted_iota(jnp.int32, (GMLP_CHUNK, GMLP_CHUNK), 0)
    c = lax.broadcasted_iota(jnp.int32, (GMLP_CHUNK, GMLP_CHUNK), 1)
    causal = c <= r
    first_group = c < GMLP_GROUP_DIM
    cols = []
    for p in range(GMLP_WIDTH // LANES):
        vp = v[:, p * LANES:(p + 1) * LANES].astype(BF16)
        w0 = jnp.where(causal, ws_ref[2 * p], 0.0).astype(BF16)
        w1 = jnp.where(causal, ws_ref[2 * p + 1], 0.0).astype(BF16)
        m0 = jnp.dot(w0, vp, preferred_element_type=F32)
        m1 = jnp.dot(w1, vp, preferred_element_type=F32)
        cols.append(jnp.where(first_group, m0, m1))
    mixed = jnp.concatenate(cols, axis=1) + bias_ref[...]
    o_ref[...] = _rms_norm(u * mixed, og_ref[...]).astype(o_ref.dtype)


def _gmlp(h, vn_g, vn_b, w_s, b_s, out_g):
    t = h.shape[0]
    u_blk = (3 * ATTN_WIDTH) // GMLP_WIDTH
    bias = jnp.repeat(b_s.T, GMLP_GROUP_DIM, axis=1)
    row = lambda a: a.reshape(1, -1)
    full = lambda shape: pl.BlockSpec(shape, lambda i: (0,) * len(shape))
    return pl.pallas_call(
        _gmlp_kernel,
        grid=(t // GMLP_CHUNK,),
        in_specs=[pl.BlockSpec((GMLP_CHUNK, GMLP_WIDTH), lambda i: (i, u_blk)),
                  pl.BlockSpec((GMLP_CHUNK, GMLP_WIDTH), lambda i: (i, u_blk + 1)),
                  full((1, GMLP_WIDTH)), full((1, GMLP_WIDTH)),
                  full((GMLP_GROUPS, GMLP_CHUNK, GMLP_CHUNK)),
                  full((GMLP_CHUNK, GMLP_WIDTH)), full((1, GMLP_WIDTH))],
        out_specs=pl.BlockSpec((GMLP_CHUNK, GMLP_WIDTH), lambda i: (i, 0)),
        out_shape=jax.ShapeDtypeStruct((t, GMLP_WIDTH), BF16),
        compiler_params=_params("parallel"),
        name="gmlp_gating",
    )(h, h, row(vn_g), row(vn_b), w_s, bias, row(out_g))


def _outproj_kernel(attn_ref, gm_ref, x_ref, ag_ref, w_ref, g_ref, b_ref, x1_ref, x1t_ref, *, alpha):
    a = _rms_norm(attn_ref[...].astype(F32), ag_ref[...]).astype(BF16)
    mix = (jnp.dot(a, w_ref[:ATTN_WIDTH, :], preferred_element_type=F32)
           + jnp.dot(gm_ref[...], w_ref[ATTN_WIDTH:, :], preferred_element_type=F32))
    x1 = _layer_norm(alpha * x_ref[...] + mix, g_ref[...], b_ref[...])
    x1_ref[...] = x1
    x1t_ref[...] = x1.T.astype(BF16)


def _outproj(attn, gm, x, attn_g, w_out, ln_g, ln_b, alpha):
    t, d = x.shape
    tm = min(256, t)
    row = lambda a: a.reshape(1, -1)
    full = lambda shape: pl.BlockSpec(shape, lambda i: (0,) * len(shape))
    return pl.pallas_call(
        functools.partial(_outproj_kernel, alpha=alpha),
        grid=(t // tm,),
        in_specs=[pl.BlockSpec((tm, ATTN_WIDTH), lambda i: (i, 0)),
                  pl.BlockSpec((tm, GMLP_WIDTH), lambda i: (i, 0)),
                  pl.BlockSpec((tm, d), lambda i: (i, 0)),
                  full((1, ATTN_WIDTH)), full(w_out.shape), full((1, d)), full((1, d))],
        out_specs=[pl.BlockSpec((tm, d), lambda i: (i, 0)),
                   pl.BlockSpec((d, tm), lambda i: (0, i))],
        out_shape=[jax.ShapeDtypeStruct((t, d), F32), jax.ShapeDtypeStruct((d, t), BF16)],
        compiler_params=_params("parallel"),
        name="outproj_ln1",
    )(attn, gm, x, row(attn_g), w_out, row(ln_g), row(ln_b))


def _cmp_exchange(x, i, j):
    hi = jnp.maximum(x[i], x[j])
    lo = jnp.minimum(x[i], x[j])
    x[i], x[j] = hi, lo


def _bitonic_merge_desc(x):
    n = len(x)
    d = n // 2
    while d >= 1:
        for i in range(n):
            if i & d == 0:
                _cmp_exchange(x, i, i | d)
        d //= 2


def _bitonic_sort_desc(x):
    n = len(x)
    k = 2
    while k <= n:
        j = k // 2
        while j >= 1:
            for i in range(n):
                l = i ^ j
                if l > i:
                    if i & k == 0:
                        _cmp_exchange(x, i, l)
                    else:
                        _cmp_exchange(x, l, i)
            j //= 2
        k *= 2


def _merge_top(a, b):
    n = len(a)
    c = [jnp.maximum(a[k], b[n - 1 - k]) for k in range(n)]
    _bitonic_merge_desc(c)
    return c


def _count_leading(pred, v):
    assert len(v) == 16
    every = pred(v[15])
    b1 = pred(v[7])
    b2 = pred(jnp.where(b1, v[11], v[3]))
    b3 = pred(jnp.where(b1, jnp.where(b2, v[13], v[9]), jnp.where(b2, v[5], v[1])))
    pair = lambda k: jnp.where(b3, v[k + 2], v[k])
    b4 = pred(jnp.where(b1, jnp.where(b2, pair(12), pair(8)), jnp.where(b2, pair(4), pair(0))))
    count = ((jnp.where(b1, 8.0, 0.0) + jnp.where(b2, 4.0, 0.0))
             + (jnp.where(b3, 2.0, 0.0) + jnp.where(b4, 1.0, 0.0)))
    return jnp.where(every, 16.0, count)


def _peer_kernel(x1t_ref, x1tb_ref, x1_ref, wqt_ref, keys_ref, u0_ref, ua_ref, ub_ref,
                 vta_ref, vtb_ref, g_ref, b_ref, o_ref,
                 s_ref, top_ref, hd_ref, e1_ref, n1_ref, e2_ref, rank2_ref,
                 ht_ref, at_ref, acc_ref, *, tm, eb, alpha):
    ti = pl.program_id(0)
    j = pl.program_id(1)
    nlt = tm // LANES
    nhp = 2 * PEER_HEADS
    nv = PEER_N_KEYS // SUBLANES

    @pl.when(j == 0)
    def _route():
        qt = jnp.dot(wqt_ref[...], x1t_ref[...], preferred_element_type=F32).astype(BF16)
        for hp in range(nhp):
            s_ref[hp] = jnp.dot(keys_ref[hp], qt[hp * PEER_N_KEYS:(hp + 1) * PEER_N_KEYS, :],
                                preferred_element_type=F32)

        def sort_body(hp, carry):
            for lt in range(nlt):
                sl = slice(lt * LANES, (lt + 1) * LANES)
                x = [s_ref[hp, v * SUBLANES:(v + 1) * SUBLANES, sl] for v in range(nv)]
                _bitonic_sort_desc(x)
                for shift in (4, 2, 1):
                    x = _merge_top(x, [pltpu.roll(xi, shift, 0) for xi in x])
                for k in range(PEER_TOPK):
                    top_ref[hp, k:k + 1, sl] = x[k][0:1, :]
            return carry

        lax.fori_loop(0, nhp, sort_body, 0)

        sub = lax.broadcasted_iota(jnp.int32, (SUBLANES, LANES), 0)
        for lt in range(nlt):
            sl = slice(lt * LANES, (lt + 1) * LANES)

            def heads_on_sublanes(p, k):
                out = jnp.zeros((SUBLANES, LANES), F32)
                for h in range(PEER_HEADS):
                    out = jnp.where(sub == h, top_ref[2 * h + p, k:k + 1, sl], out)
                return out

            v1 = [heads_on_sublanes(0, k) for k in range(PEER_TOPK)]
            v2 = [heads_on_sublanes(1, k) for k in range(PEER_TOPK)]
            c = [v1[0] + v2[k] for k in range(PEER_TOPK)]
            for r in range(1, PEER_TOPK):
                c = _merge_top(c, [v1[r] + v2[k] for k in range(PEER_TOPK)])
            z = jnp.zeros((SUBLANES, LANES), F32)
            for k in range(PEER_TOPK):
                z = z + jnp.exp(c[k] - c[0])
            per_head = (c[PEER_TOPK - 1], v1[0], v2[0], 1.0 / z)
            for q, val in enumerate(per_head):
                for h in range(PEER_HEADS):
                    hd_ref[q, h, :, sl] = val[h:h + 1, :]

        rc = PEER_N_KEYS // 2

        def head_body(h, carry):
            for lt in range(nlt):
                sl = slice(lt * LANES, (lt + 1) * LANES)
                tau, m1, m2, rz = (hd_ref[q, h, :, sl] for q in range(4))
                for rows in (slice(0, rc), slice(rc, 2 * rc)):
                    s1 = s_ref[2 * h, rows, sl]
                    s2 = s_ref[2 * h + 1, rows, sl]
                    top2 = [top_ref[2 * h + 1, r:r + 1, sl] for r in range(PEER_TOPK)]
                    rank = _count_leading(lambda v: v > s2, top2)
                    cnt = _count_leading(lambda v: s1 + v >= tau, top2)
                    n1_ref[h, rows, sl] = cnt
                    e1_ref[h, rows, sl] = jnp.exp(s1 - m1)
                    rank2_ref[h, lt, rows, :] = rank.astype(BF16)
                    e2_ref[h, lt, rows, :] = (jnp.exp(s2 - m2) * rz).astype(BF16)
            return carry

        lax.fori_loop(0, PEER_HEADS, head_body, 0)
        acc_ref[...] = jnp.zeros_like(acc_ref)

        @pl.when(ti == 0)
        def _first_block():
            ht_ref[0] = jnp.dot(u0_ref[...], x1t_ref[...], preferred_element_type=F32)

    ga = 2
    zero = jnp.zeros((PEER_N_KEYS, LANES), BF16)

    def sub_step(par, u_next_ref, xt_next_ref, vt_blk_ref):
        ht_ref[1 - par] = jnp.dot(u_next_ref[...], xt_next_ref[...], preferred_element_type=F32)
        gate(par)
        acc_ref[...] += jnp.dot(vt_blk_ref[...], at_ref[par], preferred_element_type=F32)

    def gate(par):
        a0 = pl.multiple_of((2 * j + par) * SUBLANES, SUBLANES)
        for ag in range(eb // PEER_N_KEYS // ga):
            for lt in range(nlt):
                sl = slice(lt * LANES, (lt + 1) * LANES)
                g = [zero for _ in range(ga)]
                for h in range(PEER_HEADS):
                    n1t = n1_ref[h, pl.ds(a0, SUBLANES), sl]
                    e1t = e1_ref[h, pl.ds(a0, SUBLANES), sl]
                    rank2 = rank2_ref[h, lt]
                    e2 = e2_ref[h, lt]
                    for ai in range(ga):
                        al = ag * ga + ai
                        n1 = jnp.broadcast_to(n1t[al:al + 1, :], (PEER_N_KEYS, LANES)).astype(BF16)
                        e1 = jnp.broadcast_to(e1t[al:al + 1, :], (PEER_N_KEYS, LANES)).astype(BF16)
                        g[ai] = g[ai] + jnp.where(rank2 < n1, e1 * e2, zero)
                for ai in range(ga):
                    al = ag * ga + ai
                    rows = slice(al * PEER_N_KEYS, (al + 1) * PEER_N_KEYS)
                    at_ref[par, rows, sl] = g[ai] * _gelu(ht_ref[par, rows, sl]).astype(BF16)

    sub_step(0, ua_ref, x1t_ref, vta_ref)
    sub_step(1, ub_ref, x1tb_ref, vtb_ref)

    @pl.when(j == pl.num_programs(1) - 1)
    def _finish():
        y = alpha * x1_ref[...] + acc_ref[...].T
        o_ref[...] = _layer_norm(y, g_ref[...], b_ref[...])


def _peer(x1t, x1, wqt, keys, u, vt, ln_g, ln_b, alpha):
    t, d = x1.shape
    e = u.shape[0]
    tm = min(512, t)
    eb = SUBLANES * PEER_N_KEYS
    nhp = 2 * PEER_HEADS
    nb = e // eb
    nt = t // tm
    row = lambda a: a.reshape(1, -1)
    once = pl.Buffered(1)
    full = lambda shape: pl.BlockSpec(shape, lambda i, j: (0,) * len(shape), pipeline_mode=once)
    next_tile = lambda i, j: jnp.where(j == nb // 2 - 1, jnp.minimum(i + 1, nt - 1), i)
    return pl.pallas_call(
        functools.partial(_peer_kernel, tm=tm, eb=eb, alpha=alpha),
        grid=(nt, nb // 2),
        in_specs=[pl.BlockSpec((d, tm), lambda i, j: (0, i), pipeline_mode=once),
                  pl.BlockSpec((d, tm), lambda i, j: (0, next_tile(i, j))),
                  pl.BlockSpec((tm, d), lambda i, j: (i, 0), pipeline_mode=once),
                  full(wqt.shape), full(keys.shape),
                  full((eb, d)),
                  pl.BlockSpec((eb, d), lambda i, j: (2 * j + 1, 0)),
                  pl.BlockSpec((eb, d), lambda i, j: ((2 * j + 2) % nb, 0)),
                  pl.BlockSpec((d, eb), lambda i, j: (0, 2 * j)),
                  pl.BlockSpec((d, eb), lambda i, j: (0, 2 * j + 1)),
                  full((1, d)), full((1, d))],
        out_specs=pl.BlockSpec((tm, d), lambda i, j: (i, 0)),
        out_shape=jax.ShapeDtypeStruct((t, d), F32),
        scratch_shapes=[pltpu.VMEM((nhp, PEER_N_KEYS, tm), F32),
                        pltpu.VMEM((nhp, PEER_TOPK, tm), F32),
                        pltpu.VMEM((4, PEER_HEADS, 1, tm), F32),
                        pltpu.VMEM((PEER_HEADS, PEER_N_KEYS, tm), F32),
                        pltpu.VMEM((PEER_HEADS, PEER_N_KEYS, tm), F32),
                        pltpu.VMEM((PEER_HEADS, tm // LANES, PEER_N_KEYS, LANES), BF16),
                        pltpu.VMEM((PEER_HEADS, tm // LANES, PEER_N_KEYS, LANES), BF16),
                        pltpu.VMEM((2, eb, tm), F32),
                        pltpu.VMEM((2, eb, tm), BF16),
                        pltpu.VMEM((d, tm), F32)],
        compiler_params=_params("arbitrary", "arbitrary"),
        name="peer_ln2",
    )(x1t, x1t, x1, wqt, keys, u, u, u, vt, vt, row(ln_g), row(ln_b))


def _ple_kernel(x_ref, p_ref, wg_ref, bg_ref, wp_ref, g_ref, b_ref, o_ref, *, alpha):
    x = x_ref[...]
    gate = jax.nn.sigmoid(jnp.dot(x.astype(BF16), wg_ref[...], preferred_element_type=F32)
                          + bg_ref[...])
    ple = gate * jnp.dot(p_ref[...].astype(BF16), wp_ref[...], preferred_element_type=F32)
    o_ref[...] = _layer_norm(alpha * x + ple, g_ref[...], b_ref[...])


def _ple(x, p, w_gate, b_gate, w_proj, ln_g, ln_b, alpha):
    t, d = x.shape
    tm = min(512, t)
    row = lambda a: a.reshape(1, -1)
    full = lambda shape: pl.BlockSpec(shape, lambda i: (0,) * len(shape))
    return pl.pallas_call(
        functools.partial(_ple_kernel, alpha=alpha),
        grid=(t // tm,),
        in_specs=[pl.BlockSpec((tm, d), lambda i: (i, 0)),
                  pl.BlockSpec((tm, p.shape[1]), lambda i: (i, 0)),
                  full(w_gate.shape), full((1, d)), full(w_proj.shape), full((1, d)), full((1, d))],
        out_specs=pl.BlockSpec((tm, d), lambda i: (i, 0)),
        out_shape=jax.ShapeDtypeStruct((t, d), F32),
        compiler_params=_params("parallel"),
        name="ple_ln3",
    )(x, p, w_gate, row(b_gate), w_proj, row(ln_g), row(ln_b))


def kernel(x, p, w_in, attn_out_g, gmlp_vn_g, gmlp_vn_b, gmlp_ws, gmlp_bs, gmlp_out_g, w_out, ln1_g, ln1_b, peer_wq, peer_subkeys, peer_u, peer_v, ln2_g, ln2_b, ple_wproj, ple_wgate, ple_bgate, ln3_g, ln3_b):
    b, s, d = x.shape
    depth = w_in.shape[0]
    t = b * s
    alpha = (2.0 * depth) ** 0.25
    xf = x.reshape(t, d)
    for i in range(depth):
        h = _inproj(xf, w_in[i].astype(BF16))
        attn = _attention(h.reshape(b, s, -1)).reshape(t, ATTN_WIDTH)
        gm = _gmlp(h, gmlp_vn_g[i], gmlp_vn_b[i], gmlp_ws[i], gmlp_bs[i], gmlp_out_g[i])
        x1, x1t = _outproj(attn, gm, xf, attn_out_g[i], w_out[i].astype(BF16), ln1_g[i], ln1_b[i], alpha)
        keys = peer_subkeys[i].reshape(2 * PEER_HEADS, PEER_N_KEYS, -1).astype(BF16)
        x2 = _peer(x1t, x1, peer_wq[i].T.astype(BF16), keys, peer_u[i].astype(BF16),
                   peer_v[i].T.astype(BF16), ln2_g[i], ln2_b[i], alpha)
        xf = _ple(x2, p[i].reshape(t, -1), ple_wgate[i].astype(BF16), ple_bgate[i],
                  ple_wproj[i].astype(BF16), ln3_g[i], ln3_b[i], alpha)
    return xf.reshape(b, s, d)
```

```python
import functools
import math

import jax
import jax.numpy as jnp
from jax import lax
from jax.experimental import pallas as pl
from jax.experimental.pallas import tpu as pltpu

F32 = jnp.float32
BF16 = jnp.bfloat16

ATTN_HEADS = 8
ATTN_HEAD_DIM = 64
ATTN_WIDTH = ATTN_HEADS * ATTN_HEAD_DIM
GMLP_GROUPS = 8
GMLP_GROUP_DIM = 64
GMLP_WIDTH = GMLP_GROUPS * GMLP_GROUP_DIM
GMLP_CHUNK = 128
PEER_HEADS = 8
PEER_N_KEYS = 128
PEER_TOPK = 16
LN_EPS = 1e-5

LANES = 128
SUBLANES = 8
VMEM_LIMIT = 58 * 1024 * 1024

EXP_ZERO_BELOW = -110.0


def _gelu(x):
    return 0.5 * x * (1.0 + lax.erf(x * (1.0 / math.sqrt(2.0))))


def _layer_norm(x, g, b):
    mu = jnp.mean(x, axis=-1, keepdims=True)
    xc = x - mu
    var = jnp.mean(xc * xc, axis=-1, keepdims=True)
    return xc * lax.rsqrt(var + LN_EPS) * g + b


def _rms_norm(x, g):
    return x * lax.rsqrt(jnp.mean(x * x, axis=-1, keepdims=True) + LN_EPS) * g


def _params(*sem):
    return pltpu.CompilerParams(dimension_semantics=sem, vmem_limit_bytes=VMEM_LIMIT)


def _inproj_kernel(x_ref, w_ref, o_ref):
    o_ref[...] = jnp.dot(x_ref[...].astype(BF16), w_ref[...],
                         preferred_element_type=F32).astype(o_ref.dtype)


def _inproj(x, w):
    t, d = x.shape
    n = w.shape[1]
    tm = min(512, t)
    return pl.pallas_call(
        _inproj_kernel,
        grid=(t // tm,),
        in_specs=[pl.BlockSpec((tm, d), lambda i: (i, 0)),
                  pl.BlockSpec((d, n), lambda i: (0, 0))],
        out_specs=pl.BlockSpec((tm, n), lambda i: (i, 0)),
        out_shape=jax.ShapeDtypeStruct((t, n), BF16),
        compiler_params=_params("parallel"),
        name="inproj",
    )(x, w)


def _attn_kernel(q_ref, k_ref, v_ref, o_ref, c_ref, acc_ref, *, tq, tk, nkb):
    qi = pl.program_id(2)
    q = q_ref[0] * jnp.asarray(1.0 / math.sqrt(ATTN_HEAD_DIM), BF16)
    lane = lax.broadcasted_iota(jnp.int32, (tq, LANES), 1)
    is_h0 = lane < ATTN_HEAD_DIM
    zero = jnp.zeros_like(q)
    q_heads = (jnp.where(is_h0, q, zero), jnp.where(is_h0, zero, q))
    row = lax.broadcasted_iota(jnp.int32, (tq, tk), 0) + qi * tq
    col = lax.broadcasted_iota(jnp.int32, (tq, tk), 1)
    rj = lax.broadcasted_iota(jnp.int32, (tk, tk + LANES), 0)
    cs = lax.broadcasted_iota(jnp.int32, (tk, tk + LANES), 1)
    cum = jnp.where(cs >= tk, 1.0, jnp.where(rj > cs, 1.0, 0.0)).astype(BF16)
    cum2 = jnp.concatenate([cum, cum], axis=0)

    c_ref[...] = jnp.zeros_like(c_ref)
    acc_ref[...] = jnp.zeros_like(acc_ref)
    slab = nkb * tk

    def body(carry):
        hi, _ = carry
        lo = pl.multiple_of(jnp.maximum(hi - slab, 0), tk)
        bound = jnp.minimum(row, hi) - lo
        koffs = [pl.multiple_of(lo + kb * tk, tk) for kb in range(nkb)]
        masks = [(col + kb * tk) < bound for kb in range(nkb)]
        cmax = None
        for h in range(2):
            log_beta, sums = [], []
            for kb in range(nkb):
                k = k_ref[0, pl.ds(koffs[kb], tk), :]
                z = lax.dot_general(q_heads[h], k, (((1,), (1,)), ((), ())),
                                    preferred_element_type=F32)
                sp = jnp.log(1.0 + jnp.exp(-jnp.abs(z)))
                log_beta.append(jnp.minimum(z, 0.0) - sp)
                l1m = jnp.where(masks[kb], jnp.minimum(-z, 0.0) - sp, 0.0)
                l_hi = l1m.astype(BF16)
                l_lo = (l1m - l_hi.astype(F32)).astype(BF16)
                sums.append(jnp.dot(jnp.concatenate([l_hi, l_lo], axis=1), cum2,
                                    preferred_element_type=F32))
            run = c_ref[h]
            pv = jnp.zeros((tq, LANES), F32)
            for kb in reversed(range(nkb)):
                v = v_ref[0, pl.ds(koffs[kb], tk), :]
                a = jnp.where(masks[kb], jnp.exp(log_beta[kb] + sums[kb][:, :tk] + run), 0.0)
                pv = pv + jnp.dot(a.astype(BF16), v, preferred_element_type=F32)
                run = run + sums[kb][:, tk:]
            acc_ref[h] += pv
            c_ref[h] = run
            cmax = run if cmax is None else jnp.maximum(cmax, run)
        return lo, jnp.max(cmax)

    def cond(carry):
        hi, cmax = carry
        return jnp.logical_and(hi > 0, cmax > EXP_ZERO_BELOW)

    lax.while_loop(cond, body, ((qi + 1) * tq, jnp.zeros((), F32)))
    o_ref[0] = jnp.where(is_h0, acc_ref[0], acc_ref[1]).astype(o_ref.dtype)


def _attention(h3):
    b, s, _ = h3.shape
    tq, tk, nkb = 2 * LANES, LANES, 4
    npairs = ATTN_WIDTH // LANES
    return pl.pallas_call(
        functools.partial(_attn_kernel, tq=tq, tk=tk, nkb=nkb),
        grid=(b, npairs, s // tq),
        in_specs=[pl.BlockSpec((1, tq, LANES), lambda bi, hp, qi: (bi, qi, hp)),
                  pl.BlockSpec((1, s, LANES), lambda bi, hp, qi: (bi, 0, npairs + hp)),
                  pl.BlockSpec((1, s, LANES), lambda bi, hp, qi: (bi, 0, 2 * npairs + hp))],
        out_specs=pl.BlockSpec((1, tq, LANES), lambda bi, hp, qi: (bi, qi, hp)),
        out_shape=jax.ShapeDtypeStruct((b, s, ATTN_WIDTH), BF16),
        scratch_shapes=[pltpu.VMEM((2, tq, LANES), F32), pltpu.VMEM((2, tq, LANES), F32)],
        compiler_params=_params("parallel", "parallel", "arbitrary"),
        name="sb_attention",
    )(h3, h3, h3)


def _gmlp_kernel(gu_ref, gv_ref, vng_ref, vnb_ref, ws_ref, bias_ref, og_ref, o_ref, *, nchunk):
    r = lax.broadcasted_iota(jnp.int32, (GMLP_CHUNK, GMLP_CHUNK), 0)
    c = lax.broadcasted_iota(jnp.int32, (GMLP_CHUNK, GMLP_CHUNK), 1)
    causal = c <= r
    first_group = c < GMLP_GROUP_DIM
    w = [jnp.where(causal, ws_ref[g], 0.0).astype(BF16) for g in range(GMLP_GROUPS)]
    for ck in range(nchunk):
        rows = slice(ck * GMLP_CHUNK, (ck + 1) * GMLP_CHUNK)
        u = _gelu(gu_ref[rows, :].astype(F32))
        v = _gelu(gv_ref[rows, :].astype(F32))
        v = _layer_norm(v, vng_ref[...], vnb_ref[...])
        cols = []
        for p in range(GMLP_WIDTH // LANES):
            vp = v[:, p * LANES:(p + 1) * LANES].astype(BF16)
            m0 = jnp.dot(w[2 * p], vp, preferred_element_type=F32)
            m1 = jnp.dot(w[2 * p + 1], vp, preferred_element_type=F32)
            cols.append(jnp.where(first_group, m0, m1))
        mixed = jnp.concatenate(cols, axis=1) + bias_ref[...]
        o_ref[rows, :] = _rms_norm(u * mixed, og_ref[...]).astype(o_ref.dtype)


def _gmlp(h, vn_g, vn_b, w_s, b_s, out_g):
    t = h.shape[0]
    u_blk = (3 * ATTN_WIDTH) // GMLP_WIDTH
    bias = jnp.repeat(b_s.T, GMLP_GROUP_DIM, axis=1)
    row = lambda a: a.reshape(1, -1)
    full = lambda shape: pl.BlockSpec(shape, lambda i: (0,) * len(shape))
    nchunk = 4 if t % (4 * GMLP_CHUNK) == 0 else 1
    tm = nchunk * GMLP_CHUNK
    return pl.pallas_call(
        functools.partial(_gmlp_kernel, nchunk=nchunk),
        grid=(t // tm,),
        in_specs=[pl.BlockSpec((tm, GMLP_WIDTH), lambda i: (i, u_blk)),
                  pl.BlockSpec((tm, GMLP_WIDTH), lambda i: (i, u_blk + 1)),
                  full((1, GMLP_WIDTH)), full((1, GMLP_WIDTH)),
                  full((GMLP_GROUPS, GMLP_CHUNK, GMLP_CHUNK)),
                  full((GMLP_CHUNK, GMLP_WIDTH)), full((1, GMLP_WIDTH))],
        out_specs=pl.BlockSpec((tm, GMLP_WIDTH), lambda i: (i, 0)),
        out_shape=jax.ShapeDtypeStruct((t, GMLP_WIDTH), BF16),
        compiler_params=_params("parallel"),
        name="gmlp_gating",
    )(h, h, row(vn_g), row(vn_b), w_s, bias, row(out_g))


def _outproj_kernel(attn_ref, gm_ref, x_ref, ag_ref, w_ref, g_ref, b_ref, x1_ref, x1t_ref, *, alpha):
    a = _rms_norm(attn_ref[...].astype(F32), ag_ref[...]).astype(BF16)
    mix = (jnp.dot(a, w_ref[:ATTN_WIDTH, :], preferred_element_type=F32)
           + jnp.dot(gm_ref[...], w_ref[ATTN_WIDTH:, :], preferred_element_type=F32))
    x1 = _layer_norm(alpha * x_ref[...] + mix, g_ref[...], b_ref[...])
    x1_ref[...] = x1
    x1t_ref[...] = x1.T.astype(BF16)


def _outproj(attn, gm, x, attn_g, w_out, ln_g, ln_b, alpha):
    t, d = x.shape
    tm = min(512, t)
    row = lambda a: a.reshape(1, -1)
    full = lambda shape: pl.BlockSpec(shape, lambda i: (0,) * len(shape))
    return pl.pallas_call(
        functools.partial(_outproj_kernel, alpha=alpha),
        grid=(t // tm,),
        in_specs=[pl.BlockSpec((tm, ATTN_WIDTH), lambda i: (i, 0)),
                  pl.BlockSpec((tm, GMLP_WIDTH), lambda i: (i, 0)),
                  pl.BlockSpec((tm, d), lambda i: (i, 0)),
                  full((1, ATTN_WIDTH)), full(w_out.shape), full((1, d)), full((1, d))],
        out_specs=[pl.BlockSpec((tm, d), lambda i: (i, 0)),
                   pl.BlockSpec((d, tm), lambda i: (0, i))],
        out_shape=[jax.ShapeDtypeStruct((t, d), F32), jax.ShapeDtypeStruct((d, t), BF16)],
        compiler_params=_params("parallel"),
        name="outproj_ln1",
    )(attn, gm, x, row(attn_g), w_out, row(ln_g), row(ln_b))


def _cmp_exchange(x, i, j):
    hi = jnp.maximum(x[i], x[j])
    lo = jnp.minimum(x[i], x[j])
    x[i], x[j] = hi, lo


def _bitonic_merge_desc(x):
    n = len(x)
    d = n // 2
    while d >= 1:
        for i in range(n):
            if i & d == 0:
                _cmp_exchange(x, i, i | d)
        d //= 2


def _bitonic_sort_desc(x):
    n = len(x)
    k = 2
    while k <= n:
        j = k // 2
        while j >= 1:
            for i in range(n):
                l = i ^ j
                if l > i:
                    if i & k == 0:
                        _cmp_exchange(x, i, l)
                    else:
                        _cmp_exchange(x, l, i)
            j //= 2
        k *= 2


def _merge_top(a, b):
    n = len(a)
    c = [jnp.maximum(a[k], b[n - 1 - k]) for k in range(n)]
    _bitonic_merge_desc(c)
    return c


def _count_leading(pred, v):
    assert len(v) == 16
    every = pred(v[15])
    b1 = pred(v[7])
    b2 = pred(jnp.where(b1, v[11], v[3]))
    b3 = pred(jnp.where(b1, jnp.where(b2, v[13], v[9]), jnp.where(b2, v[5], v[1])))
    pair = lambda k: jnp.where(b3, v[k + 2], v[k])
    b4 = pred(jnp.where(b1, jnp.where(b2, pair(12), pair(8)), jnp.where(b2, pair(4), pair(0))))
    count = ((jnp.where(b1, 8.0, 0.0) + jnp.where(b2, 4.0, 0.0))
             + (jnp.where(b3, 2.0, 0.0) + jnp.where(b4, 1.0, 0.0)))
    return jnp.where(every, 16.0, count)


def _peer_kernel(x1t_ref, x1tb_ref, x1_ref, wqt_ref, keys_ref, u0_ref, ua_ref, ub_ref,
                 vta_ref, vtb_ref, g_ref, b_ref, o_ref,
                 s_ref, top_ref, hd_ref, e1_ref, n1_ref, e2_ref, rank2_ref,
                 ht_ref, at_ref, acc_ref, *, tm, eb, alpha):
    ti = pl.program_id(0)
    j = pl.program_id(1)
    nlt = tm // LANES
    nhp = 2 * PEER_HEADS
    nv = PEER_N_KEYS // SUBLANES

    @pl.when(j == 0)
    def _route():
        qt = jnp.dot(wqt_ref[...], x1t_ref[...], preferred_element_type=F32).astype(BF16)
        for hp in range(nhp):
            s_ref[hp] = jnp.dot(keys_ref[hp], qt[hp * PEER_N_KEYS:(hp + 1) * PEER_N_KEYS, :],
                                preferred_element_type=F32)

        def sort_body(hp, carry):
            for lt in range(nlt):
                sl = slice(lt * LANES, (lt + 1) * LANES)
                x = [s_ref[hp, v * SUBLANES:(v + 1) * SUBLANES, sl] for v in range(nv)]
                _bitonic_sort_desc(x)
                for shift in (4, 2, 1):
                    x = _merge_top(x, [pltpu.roll(xi, shift, 0) for xi in x])
                for k in range(PEER_TOPK):
                    top_ref[hp, k:k + 1, sl] = x[k][0:1, :]
            return carry

        lax.fori_loop(0, nhp, sort_body, 0)

        sub = lax.broadcasted_iota(jnp.int32, (SUBLANES, LANES), 0)
        for lt in range(nlt):
            sl = slice(lt * LANES, (lt + 1) * LANES)

            def heads_on_sublanes(p, k):
                out = jnp.zeros((SUBLANES, LANES), F32)
                for h in range(PEER_HEADS):
                    out = jnp.where(sub == h, top_ref[2 * h + p, k:k + 1, sl], out)
                return out

            v1 = [heads_on_sublanes(0, k) for k in range(PEER_TOPK)]
            v2 = [heads_on_sublanes(1, k) for k in range(PEER_TOPK)]
            c = [v1[0] + v2[k] for k in range(PEER_TOPK)]
            for r in range(1, PEER_TOPK):
                c = _merge_top(c, [v1[r] + v2[k] for k in range(PEER_TOPK)])
            z = jnp.zeros((SUBLANES, LANES), F32)
            for k in range(PEER_TOPK):
                z = z + jnp.exp(c[k] - c[0])
            per_head = (c[PEER_TOPK - 1], v1[0], v2[0], 1.0 / z)
            for q, val in enumerate(per_head):
                for h in range(PEER_HEADS):
                    hd_ref[q, h, :, sl] = val[h:h + 1, :]

        rc = PEER_N_KEYS // 2

        def head_body(h, carry):
            for lt in range(nlt):
                sl = slice(lt * LANES, (lt + 1) * LANES)
                tau, m1, m2, rz = (hd_ref[q, h, :, sl] for q in range(4))
                for rows in (slice(0, rc), slice(rc, 2 * rc)):
                    s1 = s_ref[2 * h, rows, sl]
                    s2 = s_ref[2 * h + 1, rows, sl]
                    top2 = [top_ref[2 * h + 1, r:r + 1, sl] for r in range(PEER_TOPK)]
                    rank = _count_leading(lambda v: v > s2, top2)
                    cnt = _count_leading(lambda v: s1 + v >= tau, top2)
                    n1_ref[h, rows, sl] = cnt
                    e1_ref[h, rows, sl] = jnp.exp(s1 - m1)
                    rank2_ref[h, lt, rows, :] = rank.astype(BF16)
                    e2_ref[h, lt, rows, :] = (jnp.exp(s2 - m2) * rz).astype(BF16)
            return carry

        lax.fori_loop(0, PEER_HEADS, head_body, 0)
        acc_ref[...] = jnp.zeros_like(acc_ref)

        @pl.when(ti == 0)
        def _first_block():
            ht_ref[0] = jnp.dot(u0_ref[...], x1t_ref[...], preferred_element_type=F32)

    ga = 2
    zero = jnp.zeros((PEER_N_KEYS, LANES), BF16)

    def sub_step(par, u_next_ref, xt_next_ref, vt_blk_ref):
        ht_ref[1 - par] = jnp.dot(u_next_ref[...], xt_next_ref[...], preferred_element_type=F32)
        gate(par)
        acc_ref[...] += jnp.dot(vt_blk_ref[...], at_ref[par], preferred_element_type=F32)

    def gate(par):
        a0 = pl.multiple_of((2 * j + par) * SUBLANES, SUBLANES)
        for ag in range(eb // PEER_N_KEYS // ga):
            for lt in range(nlt):
                sl = slice(lt * LANES, (lt + 1) * LANES)
                g = [zero for _ in range(ga)]
                for h in range(PEER_HEADS):
                    n1t = n1_ref[h, pl.ds(a0, SUBLANES), sl]
                    e1t = e1_ref[h, pl.ds(a0, SUBLANES), sl]
                    rank2 = rank2_ref[h, lt]
                    e2 = e2_ref[h, lt]
                    for ai in range(ga):
                        al = ag * ga + ai
                        n1 = jnp.broadcast_to(n1t[al:al + 1, :], (PEER_N_KEYS, LANES)).astype(BF16)
                        e1 = jnp.broadcast_to(e1t[al:al + 1, :], (PEER_N_KEYS, LANES)).astype(BF16)
                        g[ai] = g[ai] + jnp.where(rank2 < n1, e1 * e2, zero)
                for ai in range(ga):
                    al = ag * ga + ai
                    rows = slice(al * PEER_N_KEYS, (al + 1) * PEER_N_KEYS)
                    at_ref[par, rows, sl] = g[ai] * _gelu(ht_ref[par, rows, sl]).astype(BF16)

    sub_step(0, ua_ref, x1t_ref, vta_ref)
    sub_step(1, ub_ref, x1tb_ref, vtb_ref)

    @pl.when(j == pl.num_programs(1) - 1)
    def _finish():
        y = alpha * x1_ref[...] + acc_ref[...].T
        o_ref[...] = _layer_norm(y, g_ref[...], b_ref[...])


def _peer(x1t, x1, wqt, keys, u, vt, ln_g, ln_b, alpha):
    t, d = x1.shape
    e = u.shape[0]
    tm = min(512, t)
    eb = SUBLANES * PEER_N_KEYS
    nhp = 2 * PEER_HEADS
    nb = e // eb
    nt = t // tm
    row = lambda a: a.reshape(1, -1)
    once = pl.Buffered(1)
    full = lambda shape: pl.BlockSpec(shape, lambda i, j: (0,) * len(shape), pipeline_mode=once)
    next_tile = lambda i, j: jnp.where(j == nb // 2 - 1, jnp.minimum(i + 1, nt - 1), i)
    return pl.pallas_call(
        functools.partial(_peer_kernel, tm=tm, eb=eb, alpha=alpha),
        grid=(nt, nb // 2),
        in_specs=[pl.BlockSpec((d, tm), lambda i, j: (0, i), pipeline_mode=once),
                  pl.BlockSpec((d, tm), lambda i, j: (0, next_tile(i, j))),
                  pl.BlockSpec((tm, d), lambda i, j: (i, 0), pipeline_mode=once),
                  full(wqt.shape), full(keys.shape),
                  full((eb, d)),
                  pl.BlockSpec((eb, d), lambda i, j: (2 * j + 1, 0)),
                  pl.BlockSpec((eb, d), lambda i, j: ((2 * j + 2) % nb, 0)),
                  pl.BlockSpec((d, eb), lambda i, j: (0, 2 * j)),
                  pl.BlockSpec((d, eb), lambda i, j: (0, 2 * j + 1)),
                  full((1, d)), full((1, d))],
        out_specs=pl.BlockSpec((tm, d), lambda i, j: (i, 0)),
        out_shape=jax.ShapeDtypeStruct((t, d), F32),
        scratch_shapes=[pltpu.VMEM((nhp, PEER_N_KEYS, tm), F32),
                        pltpu.VMEM((nhp, PEER_TOPK, tm), F32),
                        pltpu.VMEM((4, PEER_HEADS, 1, tm), F32),
                        pltpu.VMEM((PEER_HEADS, PEER_N_KEYS, tm), F32),
                        pltpu.VMEM((PEER_HEADS, PEER_N_KEYS, tm), F32),
                        pltpu.VMEM((PEER_HEADS, tm // LANES, PEER_N_KEYS, LANES), BF16),
                        pltpu.VMEM((PEER_HEADS, tm // LANES, PEER_N_KEYS, LANES), BF16),
                        pltpu.VMEM((2, eb, tm), F32),
                        pltpu.VMEM((2, eb, tm), BF16),
                        pltpu.VMEM((d, tm), F32)],
        compiler_params=_params("arbitrary", "arbitrary"),
        name="peer_ln2",
    )(x1t, x1t, x1, wqt, keys, u, u, u, vt, vt, row(ln_g), row(ln_b))


def _ple_kernel(x_ref, p_ref, wg_ref, bg_ref, wp_ref, g_ref, b_ref, o_ref, *, alpha):
    x = x_ref[...]
    gate = jax.nn.sigmoid(jnp.dot(x.astype(BF16), wg_ref[...], preferred_element_type=F32)
                          + bg_ref[...])
    ple = gate * jnp.dot(p_ref[...].astype(BF16), wp_ref[...], preferred_element_type=F32)
    o_ref[...] = _layer_norm(alpha * x + ple, g_ref[...], b_ref[...])


def _ple(x, p, w_gate, b_gate, w_proj, ln_g, ln_b, alpha):
    t, d = x.shape
    tm = min(512, t)
    row = lambda a: a.reshape(1, -1)
    full = lambda shape: pl.BlockSpec(shape, lambda i: (0,) * len(shape))
    return pl.pallas_call(
        functools.partial(_ple_kernel, alpha=alpha),
        grid=(t // tm,),
        in_specs=[pl.BlockSpec((tm, d), lambda i: (i, 0)),
                  pl.BlockSpec((tm, p.shape[1]), lambda i: (i, 0)),
                  full(w_gate.shape), full((1, d)), full(w_proj.shape), full((1, d)), full((1, d))],
        out_specs=pl.BlockSpec((tm, d), lambda i: (i, 0)),
        out_shape=jax.ShapeDtypeStruct((t, d), F32),
        compiler_params=_params("parallel"),
        name="ple_ln3",
    )(x, p, w_gate, row(b_gate), w_proj, row(ln_g), row(ln_b))


def kernel(x, p, w_in, attn_out_g, gmlp_vn_g, gmlp_vn_b, gmlp_ws, gmlp_bs, gmlp_out_g, w_out, ln1_g, ln1_b, peer_wq, peer_subkeys, peer_u, peer_v, ln2_g, ln2_b, ple_wproj, ple_wgate, ple_bgate, ln3_g, ln3_b):
    b, s, d = x.shape
    depth = w_in.shape[0]
    t = b * s
    alpha = (2.0 * depth) ** 0.25
    xf = x.reshape(t, d)
    for i in range(depth):
        h = _inproj(xf, w_in[i].astype(BF16))
        attn = _attention(h.reshape(b, s, -1)).reshape(t, ATTN_WIDTH)
        gm = _gmlp(h, gmlp_vn_g[i], gmlp_vn_b[i], gmlp_ws[i], gmlp_bs[i], gmlp_out_g[i])
        x1, x1t = _outproj(attn, gm, xf, attn_out_g[i], w_out[i].astype(BF16), ln1_g[i], ln1_b[i], alpha)
        keys = peer_subkeys[i].reshape(2 * PEER_HEADS, PEER_N_KEYS, -1).astype(BF16)
        x2 = _peer(x1t, x1, peer_wq[i].T.astype(BF16), keys, peer_u[i].astype(BF16),
                   peer_v[i].T.astype(BF16), ln2_g[i], ln2_b[i], alpha)
        xf = _ple(x2, p[i].reshape(t, -1), ple_wgate[i].astype(BF16), ple_bgate[i],
                  ple_wproj[i].astype(BF16), ln3_g[i], ln3_b[i], alpha)
    return xf.reshape(b, s, d)
```
